```python
import math
import jax, jax.numpy as jnp
from jax import lax
import numpy as np

D_MODEL = 4096
BATCH = 2
SEQ = 8192
DEPTH = 4

CHUNK = 64
SB_BLOCK = 128
N_HEADS_SB = 16
HEAD_DIM_SB = D_MODEL // 2 // N_HEADS_SB
N_GROUPS_SGU = 16
GROUP_DIM_SGU = D_MODEL // 2 // N_GROUPS_SGU
SGU_LEN = 128
N_HEADS_GDN = 32
HEAD_DIM_GDN = D_MODEL // N_HEADS_GDN
GDN_CONV = 4
D_FF = 2 * D_MODEL
FFN_CONV = 3
N_MOD = 6
N_EVEN = (DEPTH + 1) // 2
N_ODD = DEPTH // 2
EPS = 1e-6

kernel_name = "hybrid_stickbreak_sgu_gdn_streaming_encoder"


def rms_norm(x, gain):
    xf = x.astype(jnp.float32)
    y = xf * lax.rsqrt(jnp.mean(xf * xf, axis=-1, keepdims=True) + EPS)
    return (y * gain.astype(jnp.float32)).astype(x.dtype)


def l2_normalize(t):
    tf = t.astype(jnp.float32)
    return tf * lax.rsqrt(jnp.sum(tf * tf, axis=-1, keepdims=True) + EPS)


def causal_dwconv(x, w):
    K = w.shape[0]
    S_ = x.shape[1]
    xp = jnp.pad(x, ((0, 0), (K - 1, 0), (0, 0)))
    out = xp[:, 0:S_] * w[0]
    for j in range(1, K):
        out = out + xp[:, j:j + S_] * w[j]
    return out


def stick_breaking_attention(q, k, v):
    B_, S_, H_, dh = q.shape
    nb = S_ // SB_BLOCK
    qb = jnp.moveaxis(q.reshape(B_, nb, SB_BLOCK, H_, dh), 1, 0)
    key_pos = jnp.arange(S_)
    scale = dh ** -0.5

    def one_block(args):
        q_blk, blk = args
        z = jnp.einsum('bqhd,bkhd->bhqk', q_blk, k, preferred_element_type=jnp.float32) * scale
        q_pos = blk * SB_BLOCK + jnp.arange(SB_BLOCK)
        strict = key_pos[None, :] < q_pos[:, None]
        log_1m = jnp.where(strict, jax.nn.log_sigmoid(-z), 0.0)
        tail = lax.cumsum(log_1m, axis=3, reverse=True) - log_1m
        wts = jnp.where(strict, jnp.exp(jax.nn.log_sigmoid(z) + tail), 0.0)
        return jnp.einsum('bhqk,bkhd->bqhd', wts.astype(v.dtype), v)

    out = lax.map(one_block, (qb, jnp.arange(nb)))
    return jnp.moveaxis(out, 0, 1).reshape(B_, S_, H_, dh)


def spatial_gating_unit(u, v, gain, w_s, b_s):
    B_, S_, G, dg = u.shape
    u = jax.nn.gelu(u)
    v = rms_norm(jax.nn.gelu(v), gain)
    vc = v.reshape(B_, S_ // SGU_LEN, SGU_LEN, G, dg)
    causal = jnp.tril(jnp.ones((SGU_LEN, SGU_LEN), dtype=bool))
    w = jnp.where(causal, w_s, 0).astype(v.dtype)
    mixed = jnp.einsum('gts,bnsgc->bntgc', w, vc) + b_s.T[None, None, :, :, None].astype(v.dtype)
    return u * mixed.reshape(B_, S_, G, dg)


def even_mixer(h, w_in, w_out, sgu_gain, sgu_w, sgu_b):
    B_, S_, _ = h.shape
    proj = h @ w_in
    q, k, v, u, vg = jnp.split(proj, 5, axis=-1)
    sb = lambda t: t.reshape(B_, S_, N_HEADS_SB, HEAD_DIM_SB)
    sg = lambda t: t.reshape(B_, S_, N_GROUPS_SGU, GROUP_DIM_SGU)
    o_a = stick_breaking_attention(sb(q), sb(k), sb(v)).reshape(B_, S_, D_MODEL // 2)
    o_b = spatial_gating_unit(sg(u), sg(vg), sgu_gain, sgu_w, sgu_b).reshape(B_, S_, D_MODEL // 2)
    return jnp.concatenate([o_a, o_b], axis=-1) @ w_out


def chunk_gated_delta_rule(q, k, v, g, beta):
    B_, S_, H_, dk = q.shape
    dv = v.shape[-1]
    N = S_ // CHUNK

    def chunks(t):
        t = t.astype(jnp.float32).reshape((B_, N, CHUNK, H_) + t.shape[3:])
        return jnp.moveaxis(t, 3, 1)

    q, k, v, g, beta = chunks(q), chunks(k), chunks(v), chunks(g), chunks(beta)
    gc = jnp.cumsum(g, axis=-1)
    idx = jnp.arange(CHUNK)
    lower_incl = idx[:, None] >= idx[None, :]
    strict = idx[:, None] > idx[None, :]
    decay = jnp.exp(jnp.where(lower_incl, gc[..., :, None] - gc[..., None, :], -jnp.inf))
    kb = k * beta[..., None]
    a = jnp.where(strict, jnp.einsum('bhnid,bhnjd->bhnij', kb, k) * decay, 0.0)
    eye = jnp.eye(CHUNK, dtype=jnp.float32)
    t_inv = lax.linalg.triangular_solve(a + eye, jnp.broadcast_to(eye, a.shape),
                                        left_side=True, lower=True, unit_diagonal=True)
    u = jnp.einsum('bhnij,bhnjd->bhnid', t_inv, v * beta[..., None])
    w = jnp.einsum('bhnij,bhnjd->bhnid', t_inv, kb * jnp.exp(gc)[..., None])
    qk = jnp.einsum('bhnid,bhnjd->bhnij', q, k) * decay
    q_dec = q * jnp.exp(gc)[..., None]
    k_dec = k * jnp.exp(gc[..., -1:] - gc)[..., None]
    g_tot = jnp.exp(gc[..., -1])
    xs = (jnp.moveaxis(w, 2, 0), jnp.moveaxis(u, 2, 0), jnp.moveaxis(q_dec, 2, 0),
          jnp.moveaxis(qk, 2, 0), jnp.moveaxis(k_dec, 2, 0), jnp.moveaxis(g_tot, 2, 0))

    def step(state, inp):
        w_c, u_c, qd_c, qk_c, kd_c, gt_c = inp
        v_new = u_c - jnp.einsum('bhcd,bhde->bhce', w_c, state)
        o_c = jnp.einsum('bhcd,bhde->bhce', qd_c, state) + jnp.einsum('bhij,bhje->bhie', qk_c, v_new)
        state = state * gt_c[..., None, None] + jnp.einsum('bhcd,bhce->bhde', kd_c, v_new)
        return state, o_c

    state0 = jnp.zeros((B_, H_, dk, dv), jnp.float32)
    _, o = lax.scan(step, state0, xs)
    return jnp.transpose(o, (1, 0, 3, 2, 4)).reshape(B_, S_, H_, dv)


def gated_deltanet_mixer(h, w_in, conv_w, a_log, dt_bias, o_gain, w_out):
    B_, S_, _ = h.shape
    H_, dh = N_HEADS_GDN, HEAD_DIM_GDN
    proj = h @ w_in
    qkv = proj[..., :3 * D_MODEL]
    z = proj[..., 3 * D_MODEL:4 * D_MODEL]
    a = proj[..., 4 * D_MODEL:4 * D_MODEL + H_]
    b = proj[..., 4 * D_MODEL + H_:]
    qkv = jax.nn.silu(causal_dwconv(qkv, conv_w)).reshape(B_, S_, 3, H_, dh)
    q = l2_normalize(qkv[:, :, 0]) * (dh ** -0.5)
    k = l2_normalize(qkv[:, :, 1])
    v = qkv[:, :, 2]
    beta = jax.nn.sigmoid(b.astype(jnp.float32))
    g = -jnp.exp(a_log.astype(jnp.float32)) * jax.nn.softplus(a.astype(jnp.float32) + dt_bias.astype(jnp.float32))
    o = chunk_gated_delta_rule(q, k, v, g, beta)
    o = rms_norm(o, o_gain) * jax.nn.silu(z.reshape(B_, S_, H_, dh).astype(jnp.float32))
    return o.reshape(B_, S_, D_MODEL).astype(h.dtype) @ w_out


def conv_glu_ffn(h, w_up, conv_w, conv_b, w_down):
    gate, val = jnp.split(h @ w_up, 2, axis=-1)
    gate = causal_dwconv(gate, conv_w) + conv_b
    return (jax.nn.gelu(gate) * val) @ w_down


def setup_inputs(seed: int = 0) -> dict:
    key = jax.random.key(seed)
    ks = jax.random.split(key, 24)
    f32 = jnp.float32
    D = D_MODEL

    def normal(k, shape, scale):
        return jax.random.normal(k, shape, f32) * scale

    def gain(k, shape):
        return 1.0 + 0.05 * jax.random.normal(k, shape, f32)

    dt = jnp.exp(jax.random.uniform(ks[16], (N_ODD, N_HEADS_GDN), f32, math.log(1e-3), math.log(1e-1)))
    return {
        "x": normal(ks[0], (BATCH, SEQ, D), 1.0),
        "c": normal(ks[1], (BATCH, D), 1.0),
        "ada_w": normal(ks[2], (D, N_MOD * D), 0.3 * D ** -0.5),
        "ada_b": normal(ks[3], (N_MOD * D,), 0.02),
        "ada_layer": normal(ks[4], (DEPTH, N_MOD, D), 0.1),
        "norm_mix": gain(ks[5], (DEPTH, D)),
        "norm_ffn": gain(ks[6], (DEPTH, D)),
        "norm_final": gain(ks[7], (D,)),
        "ev_w_in": normal(ks[8], (N_EVEN, D, 5 * (D // 2)), D ** -0.5),
        "ev_w_out": normal(ks[9], (N_EVEN, D, D), D ** -0.5),
        "sgu_gain": gain(ks[10], (N_EVEN, N_GROUPS_SGU, GROUP_DIM_SGU)),
        "sgu_w": normal(ks[11], (N_EVEN, N_GROUPS_SGU, SGU_LEN, SGU_LEN), SGU_LEN ** -0.5),
        "sgu_b": gain(ks[12], (N_EVEN, N_GROUPS_SGU, SGU_LEN)),
        "gdn_w_in": normal(ks[13], (N_ODD, D, 4 * D + 2 * N_HEADS_GDN), D ** -0.5),
        "gdn_conv": normal(ks[14], (N_ODD, GDN_CONV, 3 * D), GDN_CONV ** -0.5),
        "gdn_a_log": jnp.log(jax.random.uniform(ks[15], (N_ODD, N_HEADS_GDN), f32, 1.0, 16.0)),
        "gdn_dt_bias": dt + jnp.log(-jnp.expm1(-dt)),
        "gdn_o_gain": gain(ks[17], (N_ODD, HEAD_DIM_GDN)),
        "gdn_w_out": normal(ks[18], (N_ODD, D, D), D ** -0.5),
        "ffn_w_up": normal(ks[19], (DEPTH, D, 2 * D_FF), D ** -0.5),
        "ffn_conv": normal(ks[20], (DEPTH, FFN_CONV, D_FF), FFN_CONV ** -0.5),
        "ffn_conv_b": normal(ks[21], (DEPTH, D_FF), 0.02),
        "ffn_w_down": normal(ks[22], (DEPTH, D_FF, D), D_FF ** -0.5),
    }


def reference(x, c, ada_w, ada_b, ada_layer, norm_mix, norm_ffn, norm_final,
              ev_w_in, ev_w_out, sgu_gain, sgu_w, sgu_b,
              gdn_w_in, gdn_conv, gdn_a_log, gdn_dt_bias, gdn_o_gain, gdn_w_out,
              ffn_w_up, ffn_conv, ffn_conv_b, ffn_w_down):
    B_ = x.shape[0]
    mod_all = (jax.nn.silu(c) @ ada_w + ada_b).reshape(B_, N_MOD, D_MODEL)
    for layer in range(DEPTH):
        mod = mod_all + ada_layer[layer]
        sh_m, sc_m, gt_m, sh_f, sc_f, gt_f = (mod[:, i, None, :] for i in range(N_MOD))
        h = rms_norm(x, norm_mix[layer]) * (1 + sc_m) + sh_m
        if layer % 2 == 0:
            e = layer // 2
            y = even_mixer(h, ev_w_in[e], ev_w_out[e], sgu_gain[e], sgu_w[e], sgu_b[e])
        else:
            o = layer // 2
            y = gated_deltanet_mixer(h, gdn_w_in[o], gdn_conv[o], gdn_a_log[o], gdn_dt_bias[o],
                                     gdn_o_gain[o], gdn_w_out[o])
        x = x + gt_m * y
        h = rms_norm(x, norm_ffn[layer]) * (1 + sc_f) + sh_f
        x = x + gt_f * conv_glu_ffn(h, ffn_w_up[layer], ffn_conv[layer], ffn_conv_b[layer], ffn_w_down[layer])
    return rms_norm(x, norm_final)
```

```python
import functools
import math

import jax
import jax.numpy as jnp
from jax import lax
from jax.experimental import pallas as pl
from jax.experimental.pallas import tpu as pltpu

F32 = jnp.float32
BF16 = jnp.bfloat16

HEAD_DIM = 128
SGU_LEN = 128
GDN_CHUNK = 64
GDN_BLOCK = 256
N_MOD = 6
EPS = 1e-6
F32_EXP_UNDERFLOW = 104.0
V7X_VMEM_BYTES = 64 * 1024 * 1024
VMEM_CAP = V7X_VMEM_BYTES - 6 * 1024 * 1024
MIB = 1024 * 1024


def _cparams(n_axes, vmem_bytes):
    limit = int(min(VMEM_CAP, max(32 * MIB, vmem_bytes * 5 // 4 + 4 * MIB)))
    return pltpu.CompilerParams(dimension_semantics=("arbitrary",) * n_axes,
                                vmem_limit_bytes=limit)


def _tile(n, pref, unit=128):
    if n <= pref:
        return n
    t = pref // unit * unit
    while n % t:
        t -= unit
    assert t > 0, (n, pref)
    return t


def _dot(a, b):
    return jnp.dot(a, b, preferred_element_type=F32)


def _dot_nt(a, b):
    return lax.dot_general(a, b, (((1,), (1,)), ((), ())), preferred_element_type=F32)


def _dot_tn(a, b):
    return lax.dot_general(a, b, (((0,), (0,)), ((), ())), preferred_element_type=F32)


def _split3(x):
    hi = x.astype(BF16)
    r = x - hi.astype(F32)
    mid = r.astype(BF16)
    lo = (r - mid.astype(F32)).astype(BF16)
    return hi, mid, lo


def _softplus(x):
    return jnp.maximum(x, 0.0) + jnp.log1p(jnp.exp(-jnp.abs(x)))


def _sigmoid(x):
    return 1.0 / (1.0 + jnp.exp(-x))


def _silu(x):
    return x * _sigmoid(x)


def _gelu(x):
    c = math.sqrt(2.0 / math.pi)
    return x * (0.5 * (1.0 + jnp.tanh(c * (x + 0.044715 * (x * x * x)))))


def _cast_kernel(w_ref, o_ref):
    o_ref[...] = w_ref[...].astype(o_ref.dtype)


def _to_bf16(w):
    shape = w.shape
    w2 = w.reshape(-1, shape[-1])
    rows, cols = w2.shape
    tr = _tile(rows, max(8, 4 * MIB // (cols * 4)), unit=8)
    out = pl.pallas_call(
        _cast_kernel,
        grid=(rows // tr,),
        in_specs=[pl.BlockSpec((tr, cols), lambda i: (i, 0))],
        out_specs=pl.BlockSpec((tr, cols), lambda i: (i, 0)),
        out_shape=jax.ShapeDtypeStruct((rows, cols), BF16),
        compiler_params=_cparams(1, 2 * tr * cols * 6),
        name="weights_to_bf16",
    )(w2)
    return out.reshape(shape)


def _ada_kernel(c_ref, w_ref, b_ref, o_ref):
    c = c_ref[...]
    o_ref[...] = _dot(_silu(c).astype(BF16), w_ref[...].astype(BF16)) + b_ref[...]


def _ada_project(c, ada_w, ada_b):
    bsz, d = c.shape
    n = ada_w.shape[1]
    rows = 8
    tn = _tile(n, 512)
    c_pad = jnp.zeros((rows, d), F32).at[:bsz].set(c)
    out = pl.pallas_call(
        _ada_kernel,
        grid=(n // tn,),
        in_specs=[pl.BlockSpec((rows, d), lambda j: (0, 0)),
                  pl.BlockSpec((d, tn), lambda j: (0, j)),
                  pl.BlockSpec((1, tn), lambda j: (0, j))],
        out_specs=pl.BlockSpec((rows, tn), lambda j: (0, j)),
        out_shape=jax.ShapeDtypeStruct((rows, n), F32),
        compiler_params=_cparams(1, 2 * d * tn * 4 + d * tn * 2),
        name="ada_project",
    )(c_pad, ada_w, ada_b.reshape(1, n))
    return out[:bsz]


def _norm_mod_kernel(x_ref, g_ref, sh_ref, sc_ref, lsh_ref, lsc_ref, o_ref):
    x = x_ref[...]
    y = x * lax.rsqrt(jnp.mean(x * x, axis=-1, keepdims=True) + EPS) * g_ref[...]
    scale = 1.0 + (sc_ref[0] + lsc_ref[0])
    shift = sh_ref[0] + lsh_ref[0]
    o_ref[...] = (y * scale + shift).astype(o_ref.dtype)


def _norm_modulate(x, gain, mod_rows, layer_rows, layer, shift_idx, seq):
    t, d = x.shape
    tr = _tile(seq, 256)
    per_seq = seq // tr
    mod_spec = lambda idx: pl.BlockSpec((1, 1, d), lambda i: ((i // per_seq) * N_MOD + idx, 0, 0))
    lay_spec = lambda idx: pl.BlockSpec((1, 1, d), lambda i: (layer * N_MOD + idx, 0, 0))
    return pl.pallas_call(
        _norm_mod_kernel,
        grid=(t // tr,),
        in_specs=[pl.BlockSpec((tr, d), lambda i: (i, 0)),
                  pl.BlockSpec((1, d), lambda i: (0, 0)),
                  mod_spec(shift_idx), mod_spec(shift_idx + 1),
                  lay_spec(shift_idx), lay_spec(shift_idx + 1)],
        out_specs=pl.BlockSpec((tr, d), lambda i: (i, 0)),
        out_shape=jax.ShapeDtypeStruct((t, d), BF16),
        compiler_params=_cparams(1, 2 * tr * d * 6),
        name="norm_modulate",
    )(x, gain.reshape(1, d), mod_rows, mod_rows, layer_rows, layer_rows)


def _final_norm_kernel(x_ref, g_ref, o_ref):
    x = x_ref[...]
    o_ref[...] = x * lax.rsqrt(jnp.mean(x * x, axis=-1, keepdims=True) + EPS) * g_ref[...]


def _final_norm(x, gain):
    t, d = x.shape
    tr = _tile(t, 256)
    return pl.pallas_call(
        _final_norm_kernel,
        grid=(t // tr,),
        in_specs=[pl.BlockSpec((tr, d), lambda i: (i, 0)),
                  pl.BlockSpec((1, d), lambda i: (0, 0))],
        out_specs=pl.BlockSpec((tr, d), lambda i: (i, 0)),
        out_shape=jax.ShapeDtypeStruct((t, d), F32),
        compiler_params=_cparams(1, 2 * tr * d * 8),
        name="final_norm",
    )(x, gain.reshape(1, d))


def _mm_plain_kernel(a_ref, w_ref, o_ref):
    o_ref[...] = _dot(a_ref[...], w_ref[...]).astype(o_ref.dtype)


def _matmul(a, w, layer, col_block0, n_out, out_dtype, tm, tn, name):
    m, k = a.shape
    out_bytes = jnp.dtype(out_dtype).itemsize
    vmem = 2 * k * tn * 2 + 2 * tm * k * 2 + 2 * tm * tn * out_bytes + tm * tn * 4
    return pl.pallas_call(
        _mm_plain_kernel,
        grid=(n_out // tn, m // tm),
        in_specs=[pl.BlockSpec((tm, k), lambda j, i: (i, 0)),
                  pl.BlockSpec((None, k, tn), lambda j, i: (layer, 0, j + col_block0))],
        out_specs=pl.BlockSpec((tm, tn), lambda j, i: (i, j)),
        out_shape=jax.ShapeDtypeStruct((m, n_out), out_dtype),
        compiler_params=_cparams(2, vmem),
        name=name,
    )(a, w)


def _mm_resid_kernel(*refs, n_parts):
    a_refs = refs[:n_parts]
    w_refs = refs[n_parts:2 * n_parts]
    x_ref, gate_ref, lgate_ref, o_ref = refs[2 * n_parts:]
    y = _dot(a_refs[0][...], w_refs[0][...])
    for a_ref, w_ref in zip(a_refs[1:], w_refs[1:]):
        y = y + _dot(a_ref[...], w_ref[...])
    o_ref[...] = x_ref[...] + (gate_ref[0] + lgate_ref[0]) * y


def _matmul_residual(a_parts, w, w_layer, x, mod_rows, layer_rows, layer, gate_idx, seq, tm, tn, name):
    n_parts = len(a_parts)
    m, kp = a_parts[0].shape
    n = w.shape[2]
    per_seq = seq // tm
    vmem = n_parts * (2 * kp * tn * 2 + 2 * tm * kp * 2) + 5 * tm * tn * 4
    a_specs = [pl.BlockSpec((tm, kp), lambda j, i: (i, 0)) for _ in range(n_parts)]
    w_specs = [pl.BlockSpec((None, kp, tn), functools.partial(lambda j, i, p: (w_layer, p, j), p=p))
               for p in range(n_parts)]
    return pl.pallas_call(
        functools.partial(_mm_resid_kernel, n_parts=n_parts),
        grid=(n // tn, m // tm),
        in_specs=a_specs + w_specs + [
            pl.BlockSpec((tm, tn), lambda j, i: (i, j)),
            pl.BlockSpec((1, 1, tn), lambda j, i: ((i // per_seq) * N_MOD + gate_idx, 0, j)),
            pl.BlockSpec((1, 1, tn), lambda j, i: (layer * N_MOD + gate_idx, 0, j))],
        out_specs=pl.BlockSpec((tm, tn), lambda j, i: (i, j)),
        out_shape=jax.ShapeDtypeStruct((m, n), F32),
        compiler_params=_cparams(2, vmem),
        name=name,
    )(*a_parts, *([w] * n_parts), x, mod_rows, layer_rows)


HALO = 8


def _causal_conv(y, ybuf_ref, cw_ref, tiles_per_seq):
    tm = y.shape[0]
    taps = cw_ref.shape[0]
    starts_sequence = (pl.program_id(1) % tiles_per_seq) == 0

    @pl.when(starts_sequence)
    def _():
        ybuf_ref[0:HALO, :] = jnp.zeros((HALO, y.shape[1]), F32)

    @pl.when(jnp.logical_not(starts_sequence))
    def _():
        ybuf_ref[0:HALO, :] = ybuf_ref[tm:tm + HALO, :]

    ybuf_ref[HALO:HALO + tm, :] = y
    out = cw_ref[taps - 1:taps, :] * y
    for j in range(taps - 1):
        off = HALO - (taps - 1) + j
        out = out + cw_ref[j:j + 1, :] * ybuf_ref[off:off + tm, :]
    return out


def _mm_ffn_up_kernel(a_ref, wg_ref, wv_ref, cw_ref, cb_ref, o_ref, ybuf_ref, *, tiles_per_seq):
    a = a_ref[...]
    gate = _causal_conv(_dot(a, wg_ref[...]), ybuf_ref, cw_ref, tiles_per_seq) + cb_ref[...]
    o_ref[...] = (_gelu(gate) * _dot(a, wv_ref[...])).astype(o_ref.dtype)


def _ffn_up(h, w_up, layer, conv_w, conv_b, seq, tm, tn):
    m, k = h.shape
    d_ff = w_up.shape[2] // 2
    nb = d_ff // tn
    vmem = 2 * 2 * k * tn * 2 + 2 * tm * k * 2 + 7 * tm * tn * 4
    return pl.pallas_call(
        functools.partial(_mm_ffn_up_kernel, tiles_per_seq=seq // tm),
        grid=(nb, m // tm),
        in_specs=[pl.BlockSpec((tm, k), lambda j, i: (i, 0)),
                  pl.BlockSpec((None, k, tn), lambda j, i: (layer, 0, j)),
                  pl.BlockSpec((None, k, tn), lambda j, i: (layer, 0, j + nb)),
                  pl.BlockSpec((conv_w.shape[0], tn), lambda j, i: (0, j)),
                  pl.BlockSpec((1, tn), lambda j, i: (0, j))],
        out_specs=pl.BlockSpec((tm, tn), lambda j, i: (i, j)),
        out_shape=jax.ShapeDtypeStruct((m, d_ff), BF16),
        scratch_shapes=[pltpu.VMEM((tm + HALO, tn), F32)],
        compiler_params=_cparams(2, vmem),
        name="ffn_up",
    )(h, w_up, w_up, conv_w, conv_b.reshape(1, d_ff))


def _mm_gdn_conv_kernel(a_ref, w_ref, cw_ref, o_ref, ybuf_ref, *, tiles_per_seq, l2_scale):
    y = _silu(_causal_conv(_dot(a_ref[...], w_ref[...]), ybuf_ref, cw_ref, tiles_per_seq))
    if l2_scale is None:
        o_ref[...] = y.astype(o_ref.dtype)
    else:
        for g in range(y.shape[1] // HEAD_DIM):
            yg = y[:, g * HEAD_DIM:(g + 1) * HEAD_DIM]
            inv = lax.rsqrt(jnp.sum(yg * yg, axis=-1, keepdims=True) + EPS)
            o_ref[:, g * HEAD_DIM:(g + 1) * HEAD_DIM] = (yg * (inv * l2_scale)).astype(o_ref.dtype)


def _gdn_conv_proj(h, w_in, layer, conv_w, col_block0, n_out, l2_scale, seq, tm, tn, name):
    m, k = h.shape
    vmem = 2 * k * tn * 2 + 2 * tm * k * 2 + 7 * tm * tn * 4
    return pl.pallas_call(
        functools.partial(_mm_gdn_conv_kernel, tiles_per_seq=seq // tm, l2_scale=l2_scale),
        grid=(n_out // tn, m // tm),
        in_specs=[pl.BlockSpec((tm, k), lambda j, i: (i, 0)),
                  pl.BlockSpec((None, k, tn), lambda j, i: (layer, 0, j + col_block0)),
                  pl.BlockSpec((conv_w.shape[0], tn), lambda j, i: (0, j + col_block0))],
        out_specs=pl.BlockSpec((tm, tn), lambda j, i: (i, j)),
        out_shape=jax.ShapeDtypeStruct((m, n_out), BF16),
        scratch_shapes=[pltpu.VMEM((tm + HALO, tn), F32)],
        compiler_params=_cparams(2, vmem),
        name=name,
    )(h, w_in, conv_w)


def _mm_gates_kernel(a_ref, w_ref, alog_ref, dt_ref, o_ref, *, n_heads):
    y = _dot(a_ref[...], w_ref[...].astype(BF16))
    g = -jnp.exp(alog_ref[...]) * _softplus(y + dt_ref[...])
    beta = _sigmoid(y)
    lane = lax.broadcasted_iota(jnp.int32, y.shape, 1)
    o_ref[...] = jnp.where(lane < n_heads, g, beta)


def _gdn_gates(h, w_ab, a_log, dt_bias, tm):
    m, k = h.shape
    n_heads = a_log.shape[0]
    tn = HEAD_DIM
    pad = lambda v: jnp.zeros((1, tn), F32).at[0, :n_heads].set(v)
    w_pad = jnp.zeros((k, tn), F32).at[:, :2 * n_heads].set(w_ab)
    vmem = 2 * k * tn * 4 + k * tn * 2 + 2 * tm * k * 2 + 4 * tm * tn * 4
    return pl.pallas_call(
        functools.partial(_mm_gates_kernel, n_heads=n_heads),
        grid=(m // tm,),
        in_specs=[pl.BlockSpec((tm, k), lambda i: (i, 0)),
                  pl.BlockSpec((k, tn), lambda i: (0, 0)),
                  pl.BlockSpec((1, tn), lambda i: (0, 0)),
                  pl.BlockSpec((1, tn), lambda i: (0, 0))],
        out_specs=pl.BlockSpec((tm, tn), lambda i: (i, 0)),
        out_shape=jax.ShapeDtypeStruct((m, tn), F32),
        compiler_params=_cparams(1, vmem),
        name="gdn_gates",
    )(h, w_pad, pad(a_log), pad(dt_bias))


def _sb_kernel(q_ref, k_ref, v_ref, o_ref, *, tq, scale):
    qi = pl.program_id(2)
    q = q_ref[...]
    row = lax.broadcasted_iota(jnp.int32, (tq, tq), 0)
    col = lax.broadcasted_iota(jnp.int32, (tq, tq), 1)
    later = (row > col).astype(BF16)
    below_diag = col < row

    def body(carry):
        kb, c, acc, _ = carry
        start = pl.multiple_of(kb * tq, tq)
        k = k_ref[pl.ds(start, tq), :]
        v = v_ref[pl.ds(start, tq), :]
        z = _dot_nt(q, k) * scale
        sp = _softplus(z)
        strict = jnp.logical_or(kb < qi, below_diag)
        log_1m = jnp.where(strict, -sp, 0.0)
        hi = log_1m.astype(BF16)
        lo = (log_1m - hi.astype(F32)).astype(BF16)
        tail = _dot(hi, later) + _dot(lo, later) + c
        w = jnp.where(strict, jnp.exp(z - sp + tail), 0.0)
        acc = acc + _dot(w.astype(BF16), v)
        c = c + jnp.sum(log_1m, axis=1, keepdims=True)
        return kb - 1, c, acc, jnp.max(c) > -F32_EXP_UNDERFLOW

    def cond(carry):
        kb, _, _, live = carry
        return jnp.logical_and(kb >= 0, live)

    init = (qi, jnp.zeros((tq, 1), F32), jnp.zeros((tq, HEAD_DIM), F32), jnp.bool_(True))
    _, _, acc, _ = lax.while_loop(cond, body, init)
    o_ref[...] = acc.astype(o_ref.dtype)


def _stick_breaking(proj, bsz, seq, n_heads, tq):
    t = proj.shape[0]
    nq = seq // tq
    vmem = 2 * 2 * seq * HEAD_DIM * 2 + 4 * tq * HEAD_DIM * 2 + 12 * tq * tq * 4
    return pl.pallas_call(
        functools.partial(_sb_kernel, tq=tq, scale=HEAD_DIM ** -0.5),
        grid=(bsz, n_heads, nq),
        in_specs=[pl.BlockSpec((tq, HEAD_DIM), lambda b, h, i: (b * nq + i, h)),
                  pl.BlockSpec((seq, HEAD_DIM), lambda b, h, i: (b, n_heads + h)),
                  pl.BlockSpec((seq, HEAD_DIM), lambda b, h, i: (b, 2 * n_heads + h))],
        out_specs=pl.BlockSpec((tq, HEAD_DIM), lambda b, h, i: (b * nq + i, h)),
        out_shape=jax.ShapeDtypeStruct((t, n_heads * HEAD_DIM), BF16),
        compiler_params=_cparams(3, vmem),
        name="stick_breaking",
    )(proj, proj, proj)


def _sgu_kernel(u_ref, v_ref, gain_ref, w_ref, b_ref, o_ref):
    rows = u_ref.shape[0]
    u = _gelu(u_ref[...].astype(F32))
    v = _gelu(v_ref[...].astype(F32))
    v = v * lax.rsqrt(jnp.mean(v * v, axis=-1, keepdims=True) + EPS) * gain_ref[0]
    r = lax.broadcasted_iota(jnp.int32, (SGU_LEN, SGU_LEN), 0)
    c = lax.broadcasted_iota(jnp.int32, (SGU_LEN, SGU_LEN), 1)
    w = jnp.where(r >= c, w_ref[0], 0.0).astype(BF16)
    bias = b_ref[0]
    for n in range(rows // SGU_LEN):
        sl = slice(n * SGU_LEN, (n + 1) * SGU_LEN)
        mixed = _dot(w, v[sl].astype(BF16)) + bias
        o_ref[sl, :] = (u[sl] * mixed).astype(o_ref.dtype)


def _spatial_gating(proj, sgu_gain, sgu_w, sgu_b, n_groups, u_block0, rows):
    t = proj.shape[0]
    return pl.pallas_call(
        _sgu_kernel,
        grid=(n_groups, t // rows),
        in_specs=[pl.BlockSpec((rows, HEAD_DIM), lambda g, i: (i, u_block0 + g)),
                  pl.BlockSpec((rows, HEAD_DIM), lambda g, i: (i, u_block0 + n_groups + g)),
                  pl.BlockSpec((1, 1, HEAD_DIM), lambda g, i: (g, 0, 0)),
                  pl.BlockSpec((1, SGU_LEN, SGU_LEN), lambda g, i: (g, 0, 0)),
                  pl.BlockSpec((1, SGU_LEN, 1), lambda g, i: (g, 0, 0))],
        out_specs=pl.BlockSpec((rows, HEAD_DIM), lambda g, i: (i, g)),
        out_shape=jax.ShapeDtypeStruct((t, n_groups * HEAD_DIM), BF16),
        compiler_params=_cparams(2, 16 * rows * HEAD_DIM * 4),
        name="spatial_gating",
    )(proj, proj, sgu_gain.reshape(n_groups, 1, HEAD_DIM), sgu_w,
      sgu_b.reshape(n_groups, SGU_LEN, 1))


def _gdn_kernel(q_ref, k_ref, v_ref, z_ref, gb_ref, gain_ref, o_ref, *, n_heads):
    h = pl.program_id(1)
    seq = q_ref.shape[0]
    blk, ch = GDN_BLOCK, GDN_CHUNK
    n_ch = blk // ch
    ch_shift = ch.bit_length() - 1

    r = lax.broadcasted_iota(jnp.int32, (blk, blk), 0)
    c = lax.broadcasted_iota(jnp.int32, (blk, blk), 1)
    same_chunk = jnp.right_shift(r, ch_shift) == jnp.right_shift(c, ch_shift)
    lower_incl = jnp.logical_and(same_chunk, r >= c)
    lower_strict = jnp.logical_and(same_chunk, r > c)
    cum_mat = lower_incl.astype(BF16)
    tot_mat = same_chunk.astype(BF16)
    lane = lax.broadcasted_iota(jnp.int32, (blk, HEAD_DIM), 1)
    gain = gain_ref[...]

    def block(b, state):
        rows = pl.ds(pl.multiple_of(b * blk, blk), blk)
        q = q_ref[rows, :].astype(F32)
        k = k_ref[rows, :].astype(F32)
        v = v_ref[rows, :].astype(F32)
        gb = gb_ref[rows, :]
        g_h = jnp.sum(jnp.where(lane == h, gb, 0.0), axis=1, keepdims=True)
        beta = jnp.sum(jnp.where(lane == h + n_heads, gb, 0.0), axis=1, keepdims=True)
        hi, mid, lo = _split3(jnp.broadcast_to(g_h, (blk, HEAD_DIM)))
        gc = _dot(cum_mat, hi) + _dot(cum_mat, mid) + _dot(cum_mat, lo)
        gl = _dot(tot_mat, hi) + _dot(tot_mat, mid) + _dot(tot_mat, lo)
        chi, cmid, clo = (part.astype(F32) for part in _split3(gc))
        left = jnp.where(lane == 0, chi, jnp.where(lane == 1, cmid, jnp.where(
            lane == 2, clo, jnp.where(lane < 6, 1.0, 0.0))))
        right = jnp.where(lane == 3, -chi, jnp.where(lane == 4, -cmid, jnp.where(
            lane == 5, -clo, jnp.where(lane < 3, 1.0, 0.0))))
        diff = _dot_nt(left.astype(BF16), right.astype(BF16))
        decay = jnp.exp(jnp.where(lower_incl, diff, -1e30))

        eg = jnp.exp(gc)
        kb = k * beta
        kbf = k.astype(BF16)
        a_mat = jnp.where(lower_strict, _dot_nt(kb.astype(BF16), kbf) * decay, 0.0)
        qk = _dot_nt(q.astype(BF16), kbf) * decay
        x = jnp.concatenate([v * beta, kb * eg], axis=1)
        p = -a_mat
        for f in range(ch_shift):
            pb = p.astype(BF16)
            x = x + _dot(pb, x.astype(BF16))
            if f + 1 < ch_shift:
                p = _dot(pb, pb)
        u = x[:, :HEAD_DIM]
        w = x[:, HEAD_DIM:].astype(BF16)
        q_dec = (q * eg).astype(BF16)
        k_dec = (k * jnp.exp(gl - gc)).astype(BF16)
        g_tot = jnp.exp(gl)

        o_inter = []
        v_new = []
        for ci in range(n_ch):
            sl = slice(ci * ch, (ci + 1) * ch)
            sb = state.astype(BF16)
            v_c = u[sl] - _dot(w[sl], sb)
            o_inter.append(_dot(q_dec[sl], sb))
            v_new.append(v_c)
            state = state * g_tot[ci * ch:ci * ch + 1, :] + _dot_tn(k_dec[sl], v_c.astype(BF16))
        v_all = jnp.concatenate(v_new, axis=0)
        o = jnp.concatenate(o_inter, axis=0) + _dot(qk.astype(BF16), v_all.astype(BF16))
        o = o * lax.rsqrt(jnp.mean(o * o, axis=-1, keepdims=True) + EPS) * gain
        o_ref[rows, :] = (o * _silu(z_ref[rows, :].astype(F32))).astype(o_ref.dtype)
        return state

    lax.fori_loop(0, seq // blk, block, jnp.zeros((HEAD_DIM, HEAD_DIM), F32))


def _gated_delta(q, k, v, z, gates, o_gain, bsz, seq, n_heads):
    t = q.shape[0]
    head_spec = pl.BlockSpec((seq, HEAD_DIM), lambda b, h: (b, h))
    vmem = 2 * (5 * seq * HEAD_DIM * 2 + seq * HEAD_DIM * 4) + 40 * GDN_BLOCK * GDN_BLOCK * 4
    return pl.pallas_call(
        functools.partial(_gdn_kernel, n_heads=n_heads),
        grid=(bsz, n_heads),
        in_specs=[head_spec, head_spec, head_spec, head_spec,
                  pl.BlockSpec((seq, HEAD_DIM), lambda b, h: (b, 0)),
                  pl.BlockSpec((1, HEAD_DIM), lambda b, h: (0, 0))],
        out_specs=head_spec,
        out_shape=jax.ShapeDtypeStruct((t, n_heads * HEAD_DIM), BF16),
        compiler_params=_cparams(2, vmem),
        name="gated_delta",
    )(q, k, v, z, gates, o_gain.reshape(1, HEAD_DIM))


def kernel(x, c, ada_w, ada_b, ada_layer, norm_mix, norm_ffn, norm_final, ev_w_in, ev_w_out, sgu_gain, sgu_w, sgu_b, gdn_w_in, gdn_conv, gdn_a_log, gdn_dt_bias, gdn_o_gain, gdn_w_out, ffn_w_up, ffn_conv, ffn_conv_b, ffn_w_down):
    bsz, seq, d = x.shape
    depth = ada_layer.shape[0]
    t = bsz * seq
    d_ff = ffn_w_down.shape[1]
    n_sb = d // 2 // HEAD_DIM
    n_sgu = sgu_gain.shape[1]
    n_gdn = gdn_a_log.shape[1]
    assert sgu_gain.shape[2] == HEAD_DIM and d == n_gdn * HEAD_DIM and sgu_w.shape[2] == SGU_LEN
    assert n_sgu == n_sb and seq % GDN_BLOCK == 0
    tm = _tile(seq, 1024)

    ev_w_in_b, ev_w_out_b = _to_bf16(ev_w_in), _to_bf16(ev_w_out)
    gdn_w_in_b, gdn_w_out_b = _to_bf16(gdn_w_in), _to_bf16(gdn_w_out)
    ffn_w_up_b, ffn_w_down_b = _to_bf16(ffn_w_up), _to_bf16(ffn_w_down)

    mod_rows = _ada_project(c, ada_w, ada_b).reshape(bsz * N_MOD, 1, d)
    layer_rows = ada_layer.reshape(depth * N_MOD, 1, d)
    x = x.reshape(t, d)

    for layer in range(depth):
        h = _norm_modulate(x, norm_mix[layer], mod_rows, layer_rows, layer, 0, seq)
        if layer % 2 == 0:
            e = layer // 2
            n_proj = 5 * (d // 2)
            proj = _matmul(h, ev_w_in_b, e, 0, n_proj, BF16, tm, _tile(n_proj, 1024), "even_in_proj")
            o_a = _stick_breaking(proj, bsz, seq, n_sb, _tile(seq, 256))
            o_b = _spatial_gating(proj, sgu_gain[e], sgu_w[e], sgu_b[e], n_sgu, 3 * n_sb,
                                  _tile(seq, 512))
            x = _matmul_residual([o_a, o_b], ev_w_out_b, e, x, mod_rows, layer_rows, layer, 2, seq,
                                 tm, _tile(d, 512), "even_out_proj")
        else:
            o = layer // 2
            tn = _tile(d, 512)
            nb = d // tn
            conv_w = gdn_conv[o]
            q = _gdn_conv_proj(h, gdn_w_in_b, o, conv_w, 0, d, HEAD_DIM ** -0.5, seq, tm, tn, "gdn_q_proj")
            k = _gdn_conv_proj(h, gdn_w_in_b, o, conv_w, nb, d, 1.0, seq, tm, tn, "gdn_k_proj")
            v = _gdn_conv_proj(h, gdn_w_in_b, o, conv_w, 2 * nb, d, None, seq, tm, tn, "gdn_v_proj")
            z = _matmul(h, gdn_w_in_b, o, 3 * nb, d, BF16, tm, tn, "gdn_z_proj")
            gates = _gdn_gates(h, gdn_w_in[o][:, 4 * d:], gdn_a_log[o], gdn_dt_bias[o], tm)
            y = _gated_delta(q, k, v, z, gates, gdn_o_gain[o], bsz, seq, n_gdn)
            x = _matmul_residual([y], gdn_w_out_b, o, x, mod_rows, layer_rows, layer, 2, seq,
                                 tm, _tile(d, 512), "gdn_out_proj")
        h = _norm_modulate(x, norm_ffn[layer], mod_rows, layer_rows, layer, 3, seq)
        f = _ffn_up(h, ffn_w_up_b, layer, ffn_conv[layer], ffn_conv_b[layer], seq, tm,
                    _tile(d_ff, 512))
        x = _matmul_residual([f], ffn_w_down_b, layer, x, mod_rows, layer_rows, layer, 5, seq,
                             _tile(seq, 512), _tile(d, 512), "ffn_down_proj")
    return _final_norm(x, norm_final).reshape(bsz, seq, d)
```

```python
import functools
import math

import jax
import jax.numpy as jnp
from jax import lax
from jax.experimental import pallas as pl
from jax.experimental.pallas import tpu as pltpu

F32 = jnp.float32
BF16 = jnp.bfloat16

HEAD_DIM = 128
SGU_LEN = 128
GDN_CHUNK = 64
GDN_BLOCK = 256
SB_HEAD_PACK = 2
GDN_HEAD_PACK = 4
N_MOD = 6
EPS = 1e-6
F32_EXP_UNDERFLOW = 104.0
V7X_VMEM_BYTES = 64 * 1024 * 1024
VMEM_CAP = V7X_VMEM_BYTES - 6 * 1024 * 1024
MIB = 1024 * 1024


def _cparams(n_axes, vmem_bytes):
    limit = int(min(VMEM_CAP, max(32 * MIB, vmem_bytes * 5 // 4 + 4 * MIB)))
    return pltpu.CompilerParams(dimension_semantics=("arbitrary",) * n_axes,
                                vmem_limit_bytes=limit)


def _tile(n, pref, unit=128):
    if n <= pref:
        return n
    t = pref // unit * unit
    while n % t:
        t -= unit
    assert t > 0, (n, pref)
    return t


def _dot(a, b):
    return jnp.dot(a, b, preferred_element_type=F32)


def _dot_nt(a, b):
    return lax.dot_general(a, b, (((1,), (1,)), ((), ())), preferred_element_type=F32)


def _dot_tn(a, b):
    return lax.dot_general(a, b, (((0,), (0,)), ((), ())), preferred_element_type=F32)


def _split3(x):
    hi = x.astype(BF16)
    r = x - hi.astype(F32)
    mid = r.astype(BF16)
    lo = (r - mid.astype(F32)).astype(BF16)
    return hi, mid, lo


def _softplus(x):
    return jnp.maximum(x, 0.0) + jnp.log1p(jnp.exp(-jnp.abs(x)))


def _sigmoid(x):
    return 1.0 / (1.0 + jnp.exp(-x))


def _silu(x):
    return x * _sigmoid(x)


def _gelu(x):
    c = math.sqrt(2.0 / math.pi)
    return x * (0.5 * (1.0 + jnp.tanh(c * (x + 0.044715 * (x * x * x)))))


def _cast_kernel(w_ref, o_ref):
    o_ref[...] = w_ref[...].astype(o_ref.dtype)


def _to_bf16(w):
    shape = w.shape
    w2 = w.reshape(-1, shape[-1])
    rows, cols = w2.shape
    tr = _tile(rows, max(8, 4 * MIB // (cols * 4)), unit=8)
    out = pl.pallas_call(
        _cast_kernel,
        grid=(rows // tr,),
        in_specs=[pl.BlockSpec((tr, cols), lambda i: (i, 0))],
        out_specs=pl.BlockSpec((tr, cols), lambda i: (i, 0)),
        out_shape=jax.ShapeDtypeStruct((rows, cols), BF16),
        compiler_params=_cparams(1, 2 * tr * cols * 6),
        name="weights_to_bf16",
    )(w2)
    return out.reshape(shape)


def _ada_kernel(c_ref, w_ref, b_ref, o_ref):
    c = c_ref[...]
    o_ref[...] = _dot(_silu(c).astype(BF16), w_ref[...].astype(BF16)) + b_ref[...]


def _ada_project(c, ada_w, ada_b):
    bsz, d = c.shape
    n = ada_w.shape[1]
    rows = 8
    tn = _tile(n, 512)
    c_pad = jnp.zeros((rows, d), F32).at[:bsz].set(c)
    out = pl.pallas_call(
        _ada_kernel,
        grid=(n // tn,),
        in_specs=[pl.BlockSpec((rows, d), lambda j: (0, 0)),
                  pl.BlockSpec((d, tn), lambda j: (0, j)),
                  pl.BlockSpec((1, tn), lambda j: (0, j))],
        out_specs=pl.BlockSpec((rows, tn), lambda j: (0, j)),
        out_shape=jax.ShapeDtypeStruct((rows, n), F32),
        compiler_params=_cparams(1, 2 * d * tn * 4 + d * tn * 2),
        name="ada_project",
    )(c_pad, ada_w, ada_b.reshape(1, n))
    return out[:bsz]


def _norm_mod_kernel(x_ref, g_ref, sh_ref, sc_ref, lsh_ref, lsc_ref, o_ref):
    x = x_ref[...]
    y = x * lax.rsqrt(jnp.mean(x * x, axis=-1, keepdims=True) + EPS) * g_ref[...]
    scale = 1.0 + (sc_ref[0] + lsc_ref[0])
    shift = sh_ref[0] + lsh_ref[0]
    o_ref[...] = (y * scale + shift).astype(o_ref.dtype)


def _norm_modulate(x, gain, mod_rows, layer_rows, layer, shift_idx, seq):
    t, d = x.shape
    tr = _tile(seq, 256)
    per_seq = seq // tr
    mod_spec = lambda idx: pl.BlockSpec((1, 1, d), lambda i: ((i // per_seq) * N_MOD + idx, 0, 0))
    lay_spec = lambda idx: pl.BlockSpec((1, 1, d), lambda i: (layer * N_MOD + idx, 0, 0))
    return pl.pallas_call(
        _norm_mod_kernel,
        grid=(t // tr,),
        in_specs=[pl.BlockSpec((tr, d), lambda i: (i, 0)),
                  pl.BlockSpec((1, d), lambda i: (0, 0)),
                  mod_spec(shift_idx), mod_spec(shift_idx + 1),
                  lay_spec(shift_idx), lay_spec(shift_idx + 1)],
        out_specs=pl.BlockSpec((tr, d), lambda i: (i, 0)),
        out_shape=jax.ShapeDtypeStruct((t, d), BF16),
        compiler_params=_cparams(1, 2 * tr * d * 6),
        name="norm_modulate",
    )(x, gain.reshape(1, d), mod_rows, mod_rows, layer_rows, layer_rows)


def _final_norm_kernel(x_ref, g_ref, o_ref):
    x = x_ref[...]
    o_ref[...] = x * lax.rsqrt(jnp.mean(x * x, axis=-1, keepdims=True) + EPS) * g_ref[...]


def _final_norm(x, gain):
    t, d = x.shape
    tr = _tile(t, 256)
    return pl.pallas_call(
        _final_norm_kernel,
        grid=(t // tr,),
        in_specs=[pl.BlockSpec((tr, d), lambda i: (i, 0)),
                  pl.BlockSpec((1, d), lambda i: (0, 0))],
        out_specs=pl.BlockSpec((tr, d), lambda i: (i, 0)),
        out_shape=jax.ShapeDtypeStruct((t, d), F32),
        compiler_params=_cparams(1, 2 * tr * d * 8),
        name="final_norm",
    )(x, gain.reshape(1, d))


def _mm_plain_kernel(a_ref, w_ref, o_ref):
    o_ref[...] = _dot(a_ref[...], w_ref[...]).astype(o_ref.dtype)


def _matmul(a, w, layer, col_block0, n_out, out_dtype, tm, tn, name):
    m, k = a.shape
    out_bytes = jnp.dtype(out_dtype).itemsize
    vmem = 2 * k * tn * 2 + 2 * tm * k * 2 + 2 * tm * tn * out_bytes + tm * tn * 4
    return pl.pallas_call(
        _mm_plain_kernel,
        grid=(n_out // tn, m // tm),
        in_specs=[pl.BlockSpec((tm, k), lambda j, i: (i, 0)),
                  pl.BlockSpec((None, k, tn), lambda j, i: (layer, 0, j + col_block0))],
        out_specs=pl.BlockSpec((tm, tn), lambda j, i: (i, j)),
        out_shape=jax.ShapeDtypeStruct((m, n_out), out_dtype),
        compiler_params=_cparams(2, vmem),
        name=name,
    )(a, w)


def _mm_resid_kernel(*refs, n_parts):
    a_refs = refs[:n_parts]
    w_refs = refs[n_parts:2 * n_parts]
    x_ref, gate_ref, lgate_ref, o_ref = refs[2 * n_parts:]
    y = _dot(a_refs[0][...], w_refs[0][...])
    for a_ref, w_ref in zip(a_refs[1:], w_refs[1:]):
        y = y + _dot(a_ref[...], w_ref[...])
    o_ref[...] = x_ref[...] + (gate_ref[0] + lgate_ref[0]) * y


def _matmul_residual(a_parts, w, w_layer, x, mod_rows, layer_rows, layer, gate_idx, seq, tm, tn, name):
    n_parts = len(a_parts)
    m, kp = a_parts[0].shape
    n = w.shape[2]
    per_seq = seq // tm
    vmem = n_parts * (2 * kp * tn * 2 + 2 * tm * kp * 2) + 5 * tm * tn * 4
    a_specs = [pl.BlockSpec((tm, kp), lambda j, i: (i, 0)) for _ in range(n_parts)]
    w_specs = [pl.BlockSpec((None, kp, tn), functools.partial(lambda j, i, p: (w_layer, p, j), p=p))
               for p in range(n_parts)]
    return pl.pallas_call(
        functools.partial(_mm_resid_kernel, n_parts=n_parts),
        grid=(n // tn, m // tm),
        in_specs=a_specs + w_specs + [
            pl.BlockSpec((tm, tn), lambda j, i: (i, j)),
            pl.BlockSpec((1, 1, tn), lambda j, i: ((i // per_seq) * N_MOD + gate_idx, 0, j)),
            pl.BlockSpec((1, 1, tn), lambda j, i: (layer * N_MOD + gate_idx, 0, j))],
        out_specs=pl.BlockSpec((tm, tn), lambda j, i: (i, j)),
        out_shape=jax.ShapeDtypeStruct((m, n), F32),
        compiler_params=_cparams(2, vmem),
        name=name,
    )(*a_parts, *([w] * n_parts), x, mod_rows, layer_rows)


HALO = 8
FFN_SUB_ROWS = 256
GDN_SUB_ROWS = 128


def _conv_begin(ybuf_ref, tm, tiles_per_seq):
    starts_sequence = (pl.program_id(1) % tiles_per_seq) == 0

    @pl.when(starts_sequence)
    def _():
        ybuf_ref[0:HALO, :] = jnp.zeros((HALO, ybuf_ref.shape[1]), F32)

    @pl.when(jnp.logical_not(starts_sequence))
    def _():
        ybuf_ref[0:HALO, :] = ybuf_ref[tm:tm + HALO, :]


def _conv_rows(y, ybuf_ref, cw_ref, r0):
    rows = y.shape[0]
    taps = cw_ref.shape[0]
    ybuf_ref[HALO + r0:HALO + r0 + rows, :] = y
    out = cw_ref[taps - 1:taps, :] * y
    for j in range(taps - 1):
        off = HALO + r0 - (taps - 1) + j
        out = out + cw_ref[j:j + 1, :] * ybuf_ref[off:off + rows, :]
    return out


def _mm_ffn_up_kernel(a_ref, wg_ref, wv_ref, cw_ref, cb_ref, o_ref, ybuf_ref, *, tiles_per_seq, sub):
    tm = a_ref.shape[0]
    _conv_begin(ybuf_ref, tm, tiles_per_seq)
    for r0 in range(0, tm, sub):
        a = a_ref[r0:r0 + sub, :]
        gate = _conv_rows(_dot(a, wg_ref[...]), ybuf_ref, cw_ref, r0) + cb_ref[...]
        o_ref[r0:r0 + sub, :] = (_gelu(gate) * _dot(a, wv_ref[...])).astype(o_ref.dtype)


def _ffn_up(h, w_up, layer, conv_w, conv_b, seq, tm, tn):
    m, k = h.shape
    d_ff = w_up.shape[2] // 2
    nb = d_ff // tn
    sub = _tile(tm, FFN_SUB_ROWS, 8)
    vmem = 2 * 2 * k * tn * 2 + 2 * tm * k * 2 + 3 * tm * tn * 4 + 8 * sub * tn * 4
    return pl.pallas_call(
        functools.partial(_mm_ffn_up_kernel, tiles_per_seq=seq // tm, sub=sub),
        grid=(nb, m // tm),
        in_specs=[pl.BlockSpec((tm, k), lambda j, i: (i, 0)),
                  pl.BlockSpec((None, k, tn), lambda j, i: (layer, 0, j)),
                  pl.BlockSpec((None, k, tn), lambda j, i: (layer, 0, j + nb)),
                  pl.BlockSpec((conv_w.shape[0], tn), lambda j, i: (0, j)),
                  pl.BlockSpec((1, tn), lambda j, i: (0, j))],
        out_specs=pl.BlockSpec((tm, tn), lambda j, i: (i, j)),
        out_shape=jax.ShapeDtypeStruct((m, d_ff), BF16),
        scratch_shapes=[pltpu.VMEM((tm + HALO, tn), F32)],
        compiler_params=_cparams(2, vmem),
        name="ffn_up",
    )(h, w_up, w_up, conv_w, conv_b.reshape(1, d_ff))


def _mm_gdn_proj_kernel(*refs, kinds, tiles_per_seq, sub):
    n = len(kinds)
    n_conv = sum(kind[0] != "plain" for kind in kinds)
    a_ref = refs[0]
    w_refs = refs[1:1 + n]
    cw_refs = iter(refs[1 + n:1 + n + n_conv])
    o_refs = refs[1 + n + n_conv:1 + 2 * n + n_conv]
    ybuf_refs = iter(refs[1 + 2 * n + n_conv:])
    conv_refs = [None if kind[0] == "plain" else (next(cw_refs), next(ybuf_refs)) for kind in kinds]
    tm = a_ref.shape[0]
    for conv in conv_refs:
        if conv is not None:
            _conv_begin(conv[1], tm, tiles_per_seq)
    for r0 in range(0, tm, sub):
        a = a_ref[r0:r0 + sub, :]
        for kind, w_ref, conv, o_ref in zip(kinds, w_refs, conv_refs, o_refs):
            y = _dot(a, w_ref[...])
            if conv is not None:
                y = _silu(_conv_rows(y, conv[1], conv[0], r0))
            if kind[0] != "l2":
                o_ref[r0:r0 + sub, :] = y.astype(o_ref.dtype)
                continue
            for g in range(y.shape[1] // HEAD_DIM):
                cols = slice(g * HEAD_DIM, (g + 1) * HEAD_DIM)
                yg = y[:, cols]
                inv = lax.rsqrt(jnp.sum(yg * yg, axis=-1, keepdims=True) + EPS)
                o_ref[r0:r0 + sub, cols] = (yg * (inv * kind[1])).astype(o_ref.dtype)


def _gdn_proj(h, w_in, layer, conv_w, streams, n_out, seq, tm, tn, name):
    m, k = h.shape
    sub = _tile(tm, GDN_SUB_ROWS, 8)
    n = len(streams)
    conv_streams = [s for s in streams if s[1][0] != "plain"]
    vmem = n * (2 * k * tn * 2 + 3 * tm * tn * 4 + 8 * sub * tn * 4) + 2 * tm * k * 2
    w_specs = [pl.BlockSpec((None, k, tn), functools.partial(lambda j, i, c0: (layer, 0, j + c0), c0=c0))
               for c0, _ in streams]
    cw_specs = [pl.BlockSpec((conv_w.shape[0], tn), functools.partial(lambda j, i, c0: (0, j + c0), c0=c0))
                for c0, _ in conv_streams]
    out_spec = pl.BlockSpec((tm, tn), lambda j, i: (i, j))
    return pl.pallas_call(
        functools.partial(_mm_gdn_proj_kernel, kinds=tuple(kind for _, kind in streams),
                          tiles_per_seq=seq // tm, sub=sub),
        grid=(n_out // tn, m // tm),
        in_specs=[pl.BlockSpec((tm, k), lambda j, i: (i, 0))] + w_specs + cw_specs,
        out_specs=[out_spec] * n,
        out_shape=[jax.ShapeDtypeStruct((m, n_out), BF16)] * n,
        scratch_shapes=[pltpu.VMEM((tm + HALO, tn), F32) for _ in conv_streams],
        compiler_params=_cparams(2, vmem),
        name=name,
    )(h, *([w_in] * n), *([conv_w] * len(conv_streams)))


def _chunk_masks(blk, ch):
    shift = ch.bit_length() - 1
    r = lax.broadcasted_iota(jnp.int32, (blk, blk), 0)
    c = lax.broadcasted_iota(jnp.int32, (blk, blk), 1)
    same_chunk = jnp.right_shift(r, shift) == jnp.right_shift(c, shift)
    return (jnp.logical_and(same_chunk, r >= c), jnp.logical_and(same_chunk, r > c), same_chunk)


def _mm_gates_kernel(a_ref, w_ref, alog_ref, dt_ref, o_ref, *, n_heads):
    tm = a_ref.shape[0]
    blk = GDN_BLOCK
    lower_incl, _, same_chunk = _chunk_masks(blk, GDN_CHUNK)
    cum_mat = lower_incl.astype(BF16)
    tot_mat = same_chunk.astype(BF16)
    lane = lax.broadcasted_iota(jnp.int32, (blk, HEAD_DIM), 1)
    w = w_ref[...]
    for r0 in range(0, tm, blk):
        y = _dot(a_ref[r0:r0 + blk, :], w)
        g = -jnp.exp(alog_ref[...]) * _softplus(y + dt_ref[...])
        hi, mid, lo = _split3(g)
        cum = _dot(cum_mat, hi) + _dot(cum_mat, mid) + _dot(cum_mat, lo)
        tot = _dot(tot_mat, hi) + _dot(tot_mat, mid) + _dot(tot_mat, lo)
        o_ref[r0:r0 + blk, :] = jnp.where(lane < n_heads, cum,
                                          jnp.where(lane < 2 * n_heads, _sigmoid(y), tot))


def _gdn_gates(h, w_ab, a_log, dt_bias, tm):
    m, k = h.shape
    n_heads = a_log.shape[0]
    tn = HEAD_DIM
    assert 3 * n_heads <= tn and tm % GDN_BLOCK == 0
    w_a, w_b = w_ab[:, :n_heads], w_ab[:, n_heads:]
    w_pad = jnp.concatenate([w_a, w_b, w_a, jnp.zeros((k, tn - 3 * n_heads), F32)], axis=1).astype(BF16)
    zeros_h = jnp.zeros((n_heads,), F32)
    pad = lambda v: jnp.concatenate([v, zeros_h, v, jnp.zeros((tn - 3 * n_heads,), F32)]).reshape(1, tn)
    vmem = 2 * k * tn * 2 + 2 * tm * k * 2 + 4 * tm * tn * 4
    return pl.pallas_call(
        functools.partial(_mm_gates_kernel, n_heads=n_heads),
        grid=(m // tm,),
        in_specs=[pl.BlockSpec((tm, k), lambda i: (i, 0)),
                  pl.BlockSpec((k, tn), lambda i: (0, 0)),
                  pl.BlockSpec((1, tn), lambda i: (0, 0)),
                  pl.BlockSpec((1, tn), lambda i: (0, 0))],
        out_specs=pl.BlockSpec((tm, tn), lambda i: (i, 0)),
        out_shape=jax.ShapeDtypeStruct((m, tn), F32),
        compiler_params=_cparams(1, vmem),
        name="gdn_gates",
    )(h, w_pad, pad(a_log), pad(dt_bias))


def _sb_kernel(q_ref, k_ref, v_ref, o_ref, *, tq, n_pack, scale):
    qi = pl.program_id(2)
    row = lax.broadcasted_iota(jnp.int32, (tq, tq), 0)
    col = lax.broadcasted_iota(jnp.int32, (tq, tq), 1)
    later = (row > col).astype(BF16)
    below_diag = col < row
    head_cols = [slice(p * HEAD_DIM, (p + 1) * HEAD_DIM) for p in range(n_pack)]
    qs = [q_ref[:, cols] for cols in head_cols]

    def body(carry):
        kb, cs, accs, _ = carry
        start = pl.multiple_of(kb * tq, tq)
        strict = jnp.logical_or(kb < qi, below_diag)
        heads = range(n_pack)
        z = [_dot_nt(qs[p], k_ref[pl.ds(start, tq), head_cols[p]]) * scale for p in heads]
        sp = [jnp.maximum(z[p], 0.0) + jnp.log(1.0 + jnp.exp(-jnp.abs(z[p]))) for p in heads]
        log_1m = [jnp.where(strict, -sp[p], 0.0) for p in heads]
        hi = [log_1m[p].astype(BF16) for p in heads]
        lo = [(log_1m[p] - hi[p].astype(F32)).astype(BF16) for p in heads]
        tail = [_dot(hi[p], later) + _dot(lo[p], later) + cs[p] for p in heads]
        w = [jnp.where(strict, jnp.exp(z[p] - sp[p] + tail[p]), 0.0).astype(BF16) for p in heads]
        new_accs = [accs[p] + _dot(w[p], v_ref[pl.ds(start, tq), head_cols[p]]) for p in heads]
        new_cs = [cs[p] + jnp.sum(log_1m[p], axis=1, keepdims=True) for p in heads]
        c_max = functools.reduce(jnp.maximum, [jnp.max(c) for c in new_cs])
        return kb - 1, tuple(new_cs), tuple(new_accs), c_max > -F32_EXP_UNDERFLOW

    def cond(carry):
        kb, _, _, live = carry
        return jnp.logical_and(kb >= 0, live)

    init = (qi, tuple(jnp.zeros((tq, 1), F32) for _ in range(n_pack)),
            tuple(jnp.zeros((tq, HEAD_DIM), F32) for _ in range(n_pack)), jnp.bool_(True))
    _, _, accs, _ = lax.while_loop(cond, body, init)
    for cols, acc in zip(head_cols, accs):
        o_ref[:, cols] = acc.astype(o_ref.dtype)


def _stick_breaking(proj, bsz, seq, n_heads, tq):
    t = proj.shape[0]
    nq = seq // tq
    n_pack = SB_HEAD_PACK if n_heads % SB_HEAD_PACK == 0 else 1
    groups = n_heads // n_pack
    width = n_pack * HEAD_DIM
    vmem = 2 * 2 * seq * width * 2 + 4 * tq * width * 2 + n_pack * 12 * tq * tq * 4
    return pl.pallas_call(
        functools.partial(_sb_kernel, tq=tq, n_pack=n_pack, scale=HEAD_DIM ** -0.5),
        grid=(bsz, groups, nq),
        in_specs=[pl.BlockSpec((tq, width), lambda b, h, i: (b * nq + i, h)),
                  pl.BlockSpec((seq, width), lambda b, h, i: (b, groups + h)),
                  pl.BlockSpec((seq, width), lambda b, h, i: (b, 2 * groups + h))],
        out_specs=pl.BlockSpec((tq, width), lambda b, h, i: (b * nq + i, h)),
        out_shape=jax.ShapeDtypeStruct((t, n_heads * HEAD_DIM), BF16),
        compiler_params=_cparams(3, vmem),
        name="stick_breaking",
    )(proj, proj, proj)


def _sgu_kernel(u_ref, v_ref, gain_ref, w_ref, b_ref, o_ref):
    rows = u_ref.shape[0]
    r = lax.broadcasted_iota(jnp.int32, (SGU_LEN, SGU_LEN), 0)
    c = lax.broadcasted_iota(jnp.int32, (SGU_LEN, SGU_LEN), 1)
    w = jnp.where(r >= c, w_ref[0], 0.0).astype(BF16)
    bias = b_ref[0]
    gain = gain_ref[0]
    for n in range(rows // SGU_LEN):
        sl = slice(n * SGU_LEN, (n + 1) * SGU_LEN)
        u = _gelu(u_ref[sl, :].astype(F32))
        v = _gelu(v_ref[sl, :].astype(F32))
        v = v * lax.rsqrt(jnp.mean(v * v, axis=-1, keepdims=True) + EPS) * gain
        mixed = _dot(w, v.astype(BF16)) + bias
        o_ref[sl, :] = (u * mixed).astype(o_ref.dtype)


def _spatial_gating(proj, sgu_gain, sgu_w, sgu_b, n_groups, u_block0, rows):
    t = proj.shape[0]
    return pl.pallas_call(
        _sgu_kernel,
        grid=(n_groups, t // rows),
        in_specs=[pl.BlockSpec((rows, HEAD_DIM), lambda g, i: (i, u_block0 + g)),
                  pl.BlockSpec((rows, HEAD_DIM), lambda g, i: (i, u_block0 + n_groups + g)),
                  pl.BlockSpec((1, 1, HEAD_DIM), lambda g, i: (g, 0, 0)),
                  pl.BlockSpec((1, SGU_LEN, SGU_LEN), lambda g, i: (g, 0, 0)),
                  pl.BlockSpec((1, SGU_LEN, 1), lambda g, i: (g, 0, 0))],
        out_specs=pl.BlockSpec((rows, HEAD_DIM), lambda g, i: (i, g)),
        out_shape=jax.ShapeDtypeStruct((t, n_groups * HEAD_DIM), BF16),
        compiler_params=_cparams(2, 16 * rows * HEAD_DIM * 4),
        name="spatial_gating",
    )(proj, proj, sgu_gain.reshape(n_groups, 1, HEAD_DIM), sgu_w,
      sgu_b.reshape(n_groups, SGU_LEN, 1))


def _gdn_kernel(q_ref, k_ref, v_ref, z_ref, gb_ref, gain_ref, o_ref, state_ref, *, n_heads, n_pack):
    hg = pl.program_id(1)
    n_rows = q_ref.shape[0]
    blk, ch = GDN_BLOCK, GDN_CHUNK
    n_ch = blk // ch
    n_factors = ch.bit_length() - 1

    @pl.when(pl.program_id(2) == 0)
    def _():
        state_ref[...] = jnp.zeros(state_ref.shape, F32)

    lower_incl, lower_strict, _ = _chunk_masks(blk, ch)
    lane = lax.broadcasted_iota(jnp.int32, (blk, HEAD_DIM), 1)
    gain = gain_ref[...]

    def pick(gb, lane_idx):
        return jnp.sum(jnp.where(lane == lane_idx, gb, 0.0), axis=1, keepdims=True)

    def diff_operands(gc):
        chi, cmid, clo = (part.astype(F32) for part in _split3(jnp.broadcast_to(gc, (blk, HEAD_DIM))))
        left = jnp.where(lane == 0, chi, jnp.where(lane == 1, cmid, jnp.where(
            lane == 2, clo, jnp.where(lane < 6, 1.0, 0.0))))
        right = jnp.where(lane == 3, -chi, jnp.where(lane == 4, -cmid, jnp.where(
            lane == 5, -clo, jnp.where(lane < 3, 1.0, 0.0))))
        return left.astype(BF16), right.astype(BF16)

    head_cols = [slice(p * HEAD_DIM, (p + 1) * HEAD_DIM) for p in range(n_pack)]
    heads = range(n_pack)

    def block(b, states):
        rows = pl.ds(pl.multiple_of(b * blk, blk), blk)
        gb = gb_ref[rows, :]
        q = [q_ref[rows, cols].astype(F32) for cols in head_cols]
        k = [k_ref[rows, cols].astype(F32) for cols in head_cols]
        v = [v_ref[rows, cols].astype(F32) for cols in head_cols]
        gc = [pick(gb, hg * n_pack + p) for p in heads]
        beta = [pick(gb, hg * n_pack + p + n_heads) for p in heads]
        gl = [pick(gb, hg * n_pack + p + 2 * n_heads) for p in heads]
        diff = [_dot_nt(*diff_operands(g)) for g in gc]
        decay = [jnp.exp(jnp.where(lower_incl, d, -1e30)) for d in diff]
        eg = [jnp.exp(g) for g in gc]
        kb = [k[p] * beta[p] for p in heads]
        kbf = [k[p].astype(BF16) for p in heads]
        p_mat = [jnp.where(lower_strict, _dot_nt((-kb[p]).astype(BF16), kbf[p]) * decay[p], 0.0)
                 for p in heads]
        qk = [(_dot_nt(q[p].astype(BF16), kbf[p]) * decay[p]).astype(BF16) for p in heads]
        x = [jnp.concatenate([v[p] * beta[p], kb[p] * eg[p]], axis=1) for p in heads]
        for f in range(n_factors):
            pb = [p_mat[p].astype(BF16) for p in heads]
            x = [x[p] + _dot(pb[p], x[p].astype(BF16)) for p in heads]
            if f + 1 < n_factors:
                p_mat = [_dot(pb[p], pb[p]) for p in heads]
        u = [x[p][:, :HEAD_DIM] for p in heads]
        w = [x[p][:, HEAD_DIM:].astype(BF16) for p in heads]
        q_dec = [(q[p] * eg[p]).astype(BF16) for p in heads]
        k_dec = [(k[p] * jnp.exp(gl[p] - gc[p])).astype(BF16) for p in heads]
        g_tot = [jnp.exp(gl[p]) for p in heads]

        states = list(states)
        o_inter = [[] for _ in heads]
        v_new = [[] for _ in heads]
        for ci in range(n_ch):
            sl = slice(ci * ch, (ci + 1) * ch)
            sb = [states[p].astype(BF16) for p in heads]
            v_c = [u[p][sl] - _dot(w[p][sl], sb[p]) for p in heads]
            for p in heads:
                o_inter[p].append(_dot(q_dec[p][sl], sb[p]))
                v_new[p].append(v_c[p])
            states = [states[p] * g_tot[p][ci * ch:ci * ch + 1, :]
                      + _dot_tn(k_dec[p][sl], v_c[p].astype(BF16)) for p in heads]
        o = [jnp.concatenate(o_inter[p], axis=0)
             + _dot(qk[p], jnp.concatenate(v_new[p], axis=0).astype(BF16)) for p in heads]
        for p, cols in enumerate(head_cols):
            on = o[p] * lax.rsqrt(jnp.mean(o[p] * o[p], axis=-1, keepdims=True) + EPS) * gain
            o_ref[rows, cols] = (on * _silu(z_ref[rows, cols].astype(F32))).astype(o_ref.dtype)
        return tuple(states)

    states = lax.fori_loop(0, n_rows // blk, block, tuple(state_ref[p] for p in range(n_pack)))
    for p in range(n_pack):
        state_ref[p] = states[p]


def _gated_delta(q, k, v, z, gates, o_gain, bsz, seq, n_heads):
    t = q.shape[0]
    n_pack = GDN_HEAD_PACK if n_heads % GDN_HEAD_PACK == 0 else 1
    width = n_pack * HEAD_DIM
    rows = _tile(seq, 2048, GDN_BLOCK)
    tiles = seq // rows
    head_spec = pl.BlockSpec((rows, width), lambda b, h, r: (b * tiles + r, h))
    vmem = 2 * (5 * rows * width * 2 + rows * HEAD_DIM * 4) + n_pack * 40 * GDN_BLOCK * GDN_BLOCK * 4
    return pl.pallas_call(
        functools.partial(_gdn_kernel, n_heads=n_heads, n_pack=n_pack),
        grid=(bsz, n_heads // n_pack, tiles),
        in_specs=[head_spec, head_spec, head_spec, head_spec,
                  pl.BlockSpec((rows, HEAD_DIM), lambda b, h, r: (b * tiles + r, 0)),
                  pl.BlockSpec((1, HEAD_DIM), lambda b, h, r: (0, 0))],
        out_specs=head_spec,
        out_shape=jax.ShapeDtypeStruct((t, n_heads * HEAD_DIM), BF16),
        scratch_shapes=[pltpu.VMEM((n_pack, HEAD_DIM, HEAD_DIM), F32)],
        compiler_params=_cparams(3, vmem),
        name="gated_delta",
    )(q, k, v, z, gates, o_gain.reshape(1, HEAD_DIM))


def kernel(x, c, ada_w, ada_b, ada_layer, norm_mix, norm_ffn, norm_final, ev_w_in, ev_w_out, sgu_gain, sgu_w, sgu_b, gdn_w_in, gdn_conv, gdn_a_log, gdn_dt_bias, gdn_o_gain, gdn_w_out, ffn_w_up, ffn_conv, ffn_conv_b, ffn_w_down):
    bsz, seq, d = x.shape
    depth = ada_layer.shape[0]
    t = bsz * seq
    d_ff = ffn_w_down.shape[1]
    n_sb = d // 2 // HEAD_DIM
    n_sgu = sgu_gain.shape[1]
    n_gdn = gdn_a_log.shape[1]
    assert sgu_gain.shape[2] == HEAD_DIM and d == n_gdn * HEAD_DIM and sgu_w.shape[2] == SGU_LEN
    assert n_sgu == n_sb and seq % GDN_BLOCK == 0
    tm = _tile(seq, 1024)

    ev_w_in_b, ev_w_out_b = _to_bf16(ev_w_in), _to_bf16(ev_w_out)
    gdn_w_in_b, gdn_w_out_b = _to_bf16(gdn_w_in), _to_bf16(gdn_w_out)
    ffn_w_up_b, ffn_w_down_b = _to_bf16(ffn_w_up), _to_bf16(ffn_w_down)
    gdn_w_ab = gdn_w_in[:, :, 4 * d:]

    mod_rows = _ada_project(c, ada_w, ada_b).reshape(bsz * N_MOD, 1, d)
    layer_rows = ada_layer.reshape(depth * N_MOD, 1, d)
    x = x.reshape(t, d)

    for layer in range(depth):
        h = _norm_modulate(x, norm_mix[layer], mod_rows, layer_rows, layer, 0, seq)
        if layer % 2 == 0:
            e = layer // 2
            n_proj = 5 * (d // 2)
            proj = _matmul(h, ev_w_in_b, e, 0, n_proj, BF16, tm, _tile(n_proj, 1024), "even_in_proj")
            o_a = _stick_breaking(proj, bsz, seq, n_sb, _tile(seq, 256))
            o_b = _spatial_gating(proj, sgu_gain[e], sgu_w[e], sgu_b[e], n_sgu, 3 * n_sb,
                                  _tile(seq, 2048))
            x = _matmul_residual([o_a, o_b], ev_w_out_b, e, x, mod_rows, layer_rows, layer, 2, seq,
                                 tm, _tile(d, 512), "even_out_proj")
        else:
            o = layer // 2
            tn = _tile(d, 512)
            nb = d // tn
            conv_w = gdn_conv[o]
            q, k = _gdn_proj(h, gdn_w_in_b, o, conv_w, [(0, ("l2", HEAD_DIM ** -0.5)), (nb, ("l2", 1.0))],
                             d, seq, tm, tn, "gdn_qk_proj")
            v, z = _gdn_proj(h, gdn_w_in_b, o, conv_w, [(2 * nb, ("silu",)), (3 * nb, ("plain",))],
                             d, seq, tm, tn, "gdn_vz_proj")
            gates = _gdn_gates(h, gdn_w_ab[o], gdn_a_log[o], gdn_dt_bias[o], tm)
            y = _gated_delta(q, k, v, z, gates, gdn_o_gain[o], bsz, seq, n_gdn)
            x = _matmul_residual([y], gdn_w_out_b, o, x, mod_rows, layer_rows, layer, 2, seq,
                                 tm, _tile(d, 512), "gdn_out_proj")
        h = _norm_modulate(x, norm_ffn[layer], mod_rows, layer_rows, layer, 3, seq)
        f = _ffn_up(h, ffn_w_up_b, layer, ffn_conv[layer], ffn_conv_b[layer], seq, tm,
                    _tile(d_ff, 512))
        x = _matmul_residual([f], ffn_w_down_b, layer, x, mod_rows, layer_rows, layer, 5, seq,
                             _tile(seq, 512), _tile(d, 512), "ffn_down_proj")
    return _final_norm(x, norm_final).reshape(bsz, seq, d)
```

```python
import functools
import math

import jax
import jax.numpy as jnp
from jax import lax
from jax.experimental import pallas as pl
from jax.experimental.pallas import tpu as pltpu

F32 = jnp.float32
BF16 = jnp.bfloat16

HEAD_DIM = 128
SGU_LEN = 128
GDN_CHUNK = 64
GDN_BLOCK = 256
SB_HEAD_PACK = 2
GDN_HEAD_PACK = 4
N_MOD = 6
EPS = 1e-6
F32_EXP_UNDERFLOW = 104.0
V7X_VMEM_BYTES = 64 * 1024 * 1024
VMEM_CAP = V7X_VMEM_BYTES - 6 * 1024 * 1024
MIB = 1024 * 1024


def _cparams(n_axes, vmem_bytes):
    limit = int(min(VMEM_CAP, max(32 * MIB, vmem_bytes * 5 // 4 + 4 * MIB)))
    return pltpu.CompilerParams(dimension_semantics=("arbitrary",) * n_axes,
                                vmem_limit_bytes=limit)


def _tile(n, pref, unit=128):
    if n <= pref:
        return n
    t = pref // unit * unit
    while n % t:
        t -= unit
    assert t > 0, (n, pref)
    return t


def _dot(a, b):
    return jnp.dot(a, b, preferred_element_type=F32)


def _dot_nt(a, b):
    return lax.dot_general(a, b, (((1,), (1,)), ((), ())), preferred_element_type=F32)


def _dot_tn(a, b):
    return lax.dot_general(a, b, (((0,), (0,)), ((), ())), preferred_element_type=F32)


def _split3(x):
    hi = x.astype(BF16)
    r = x - hi.astype(F32)
    mid = r.astype(BF16)
    lo = (r - mid.astype(F32)).astype(BF16)
    return hi, mid, lo


def _softplus(x):
    return jnp.maximum(x, 0.0) + jnp.log1p(jnp.exp(-jnp.abs(x)))


def _sigmoid(x):
    return 1.0 / (1.0 + jnp.exp(-x))


def _silu(x):
    return x * _sigmoid(x)


def _gelu(x):
    c = math.sqrt(2.0 / math.pi)
    return x * (0.5 * (1.0 + jnp.tanh(c * (x + 0.044715 * (x * x * x)))))


def _cast_kernel(w_ref, o_ref):
    o_ref[...] = w_ref[...].astype(o_ref.dtype)


def _to_bf16(w):
    shape = w.shape
    w2 = w.reshape(-1, shape[-1])
    rows, cols = w2.shape
    tr = _tile(rows, max(8, 4 * MIB // (cols * 4)), unit=8)
    out = pl.pallas_call(
        _cast_kernel,
        grid=(rows // tr,),
        in_specs=[pl.BlockSpec((tr, cols), lambda i: (i, 0))],
        out_specs=pl.BlockSpec((tr, cols), lambda i: (i, 0)),
        out_shape=jax.ShapeDtypeStruct((rows, cols), BF16),
        compiler_params=_cparams(1, 2 * tr * cols * 6),
        name="weights_to_bf16",
    )(w2)
    return out.reshape(shape)


def _ada_kernel(c_ref, w_ref, b_ref, o_ref):
    c = c_ref[...]
    o_ref[...] = _dot(_silu(c).astype(BF16), w_ref[...].astype(BF16)) + b_ref[...]


def _ada_project(c, ada_w, ada_b):
    bsz, d = c.shape
    n = ada_w.shape[1]
    rows = 8
    tn = _tile(n, 512)
    c_pad = jnp.zeros((rows, d), F32).at[:bsz].set(c)
    out = pl.pallas_call(
        _ada_kernel,
        grid=(n // tn,),
        in_specs=[pl.BlockSpec((rows, d), lambda j: (0, 0)),
                  pl.BlockSpec((d, tn), lambda j: (0, j)),
                  pl.BlockSpec((1, tn), lambda j: (0, j))],
        out_specs=pl.BlockSpec((rows, tn), lambda j: (0, j)),
        out_shape=jax.ShapeDtypeStruct((rows, n), F32),
        compiler_params=_cparams(1, 2 * d * tn * 4 + d * tn * 2),
        name="ada_project",
    )(c_pad, ada_w, ada_b.reshape(1, n))
    return out[:bsz]


def _norm_mod_kernel(x_ref, g_ref, sh_ref, sc_ref, lsh_ref, lsc_ref, o_ref):
    x = x_ref[...]
    y = x * lax.rsqrt(jnp.mean(x * x, axis=-1, keepdims=True) + EPS) * g_ref[...]
    scale = 1.0 + (sc_ref[0] + lsc_ref[0])
    shift = sh_ref[0] + lsh_ref[0]
    o_ref[...] = (y * scale + shift).astype(o_ref.dtype)


def _norm_modulate(x, gain, mod_rows, layer_rows, layer, shift_idx, seq):
    t, d = x.shape
    tr = _tile(seq, 512)
    per_seq = seq // tr
    mod_spec = lambda idx: pl.BlockSpec((1, 1, d), lambda i: ((i // per_seq) * N_MOD + idx, 0, 0))
    lay_spec = lambda idx: pl.BlockSpec((1, 1, d), lambda i: (layer * N_MOD + idx, 0, 0))
    return pl.pallas_call(
        _norm_mod_kernel,
        grid=(t // tr,),
        in_specs=[pl.BlockSpec((tr, d), lambda i: (i, 0)),
                  pl.BlockSpec((1, d), lambda i: (0, 0)),
                  mod_spec(shift_idx), mod_spec(shift_idx + 1),
                  lay_spec(shift_idx), lay_spec(shift_idx + 1)],
        out_specs=pl.BlockSpec((tr, d), lambda i: (i, 0)),
        out_shape=jax.ShapeDtypeStruct((t, d), BF16),
        compiler_params=_cparams(1, 2 * tr * d * 6),
        name="norm_modulate",
    )(x, gain.reshape(1, d), mod_rows, mod_rows, layer_rows, layer_rows)


def _final_norm_kernel(x_ref, g_ref, o_ref):
    x = x_ref[...]
    o_ref[...] = x * lax.rsqrt(jnp.mean(x * x, axis=-1, keepdims=True) + EPS) * g_ref[...]


def _final_norm(x, gain):
    t, d = x.shape
    tr = _tile(t, 256)
    return pl.pallas_call(
        _final_norm_kernel,
        grid=(t // tr,),
        in_specs=[pl.BlockSpec((tr, d), lambda i: (i, 0)),
                  pl.BlockSpec((1, d), lambda i: (0, 0))],
        out_specs=pl.BlockSpec((tr, d), lambda i: (i, 0)),
        out_shape=jax.ShapeDtypeStruct((t, d), F32),
        compiler_params=_cparams(1, 2 * tr * d * 8),
        name="final_norm",
    )(x, gain.reshape(1, d))


def _mm_plain_kernel(a_ref, w_ref, o_ref):
    o_ref[...] = _dot(a_ref[...], w_ref[...]).astype(o_ref.dtype)


def _matmul(a, w, layer, col_block0, n_out, out_dtype, tm, tn, name):
    m, k = a.shape
    out_bytes = jnp.dtype(out_dtype).itemsize
    vmem = 2 * k * tn * 2 + 2 * tm * k * 2 + 2 * tm * tn * out_bytes + tm * tn * 4
    return pl.pallas_call(
        _mm_plain_kernel,
        grid=(n_out // tn, m // tm),
        in_specs=[pl.BlockSpec((tm, k), lambda j, i: (i, 0)),
                  pl.BlockSpec((None, k, tn), lambda j, i: (layer, 0, j + col_block0))],
        out_specs=pl.BlockSpec((tm, tn), lambda j, i: (i, j)),
        out_shape=jax.ShapeDtypeStruct((m, n_out), out_dtype),
        compiler_params=_cparams(2, vmem),
        name=name,
    )(a, w)


def _mm_resid_kernel(*refs, n_parts):
    a_refs = refs[:n_parts]
    w_refs = refs[n_parts:2 * n_parts]
    x_ref, gate_ref, lgate_ref, o_ref = refs[2 * n_parts:]
    y = _dot(a_refs[0][...], w_refs[0][...])
    for a_ref, w_ref in zip(a_refs[1:], w_refs[1:]):
        y = y + _dot(a_ref[...], w_ref[...])
    o_ref[...] = x_ref[...] + (gate_ref[0] + lgate_ref[0]) * y


def _matmul_residual(a_parts, w, w_layer, x, mod_rows, layer_rows, layer, gate_idx, seq, tm, tn, name):
    n_parts = len(a_parts)
    m, kp = a_parts[0].shape
    n = w.shape[2]
    per_seq = seq // tm
    vmem = n_parts * (2 * kp * tn * 2 + 2 * tm * kp * 2) + 5 * tm * tn * 4
    a_specs = [pl.BlockSpec((tm, kp), lambda j, i: (i, 0)) for _ in range(n_parts)]
    w_specs = [pl.BlockSpec((None, kp, tn), functools.partial(lambda j, i, p: (w_layer, p, j), p=p))
               for p in range(n_parts)]
    return pl.pallas_call(
        functools.partial(_mm_resid_kernel, n_parts=n_parts),
        grid=(n // tn, m // tm),
        in_specs=a_specs + w_specs + [
            pl.BlockSpec((tm, tn), lambda j, i: (i, j)),
            pl.BlockSpec((1, 1, tn), lambda j, i: ((i // per_seq) * N_MOD + gate_idx, 0, j)),
            pl.BlockSpec((1, 1, tn), lambda j, i: (layer * N_MOD + gate_idx, 0, j))],
        out_specs=pl.BlockSpec((tm, tn), lambda j, i: (i, j)),
        out_shape=jax.ShapeDtypeStruct((m, n), F32),
        compiler_params=_cparams(2, vmem),
        name=name,
    )(*a_parts, *([w] * n_parts), x, mod_rows, layer_rows)


HALO = 8
FFN_SUB_ROWS = 256
GDN_SUB_ROWS = 128


def _conv_begin(ybuf_ref, tm, tiles_per_seq):
    starts_sequence = (pl.program_id(1) % tiles_per_seq) == 0

    @pl.when(starts_sequence)
    def _():
        ybuf_ref[0:HALO, :] = jnp.zeros((HALO, ybuf_ref.shape[1]), F32)

    @pl.when(jnp.logical_not(starts_sequence))
    def _():
        ybuf_ref[0:HALO, :] = ybuf_ref[tm:tm + HALO, :]


def _conv_rows(y, ybuf_ref, cw_ref, r0):
    rows = y.shape[0]
    taps = cw_ref.shape[0]
    ybuf_ref[HALO + r0:HALO + r0 + rows, :] = y
    out = cw_ref[taps - 1:taps, :] * y
    for j in range(taps - 1):
        off = HALO + r0 - (taps - 1) + j
        out = out + cw_ref[j:j + 1, :] * ybuf_ref[off:off + rows, :]
    return out


def _mm_ffn_up_kernel(a_ref, wg_ref, wv_ref, cw_ref, cb_ref, o_ref, ybuf_ref, *, tiles_per_seq, sub):
    tm = a_ref.shape[0]
    _conv_begin(ybuf_ref, tm, tiles_per_seq)
    for r0 in range(0, tm, sub):
        a = a_ref[r0:r0 + sub, :]
        gate = _conv_rows(_dot(a, wg_ref[...]), ybuf_ref, cw_ref, r0) + cb_ref[...]
        o_ref[r0:r0 + sub, :] = (_gelu(gate) * _dot(a, wv_ref[...])).astype(o_ref.dtype)


def _ffn_up(h, w_up, layer, conv_w, conv_b, seq, tm, tn):
    m, k = h.shape
    d_ff = w_up.shape[2] // 2
    nb = d_ff // tn
    sub = _tile(tm, FFN_SUB_ROWS, 8)
    vmem = 2 * 2 * k * tn * 2 + 2 * tm * k * 2 + 3 * tm * tn * 4 + 8 * sub * tn * 4
    return pl.pallas_call(
        functools.partial(_mm_ffn_up_kernel, tiles_per_seq=seq // tm, sub=sub),
        grid=(nb, m // tm),
        in_specs=[pl.BlockSpec((tm, k), lambda j, i: (i, 0)),
                  pl.BlockSpec((None, k, tn), lambda j, i: (layer, 0, j)),
                  pl.BlockSpec((None, k, tn), lambda j, i: (layer, 0, j + nb)),
                  pl.BlockSpec((conv_w.shape[0], tn), lambda j, i: (0, j)),
                  pl.BlockSpec((1, tn), lambda j, i: (0, j))],
        out_specs=pl.BlockSpec((tm, tn), lambda j, i: (i, j)),
        out_shape=jax.ShapeDtypeStruct((m, d_ff), BF16),
        scratch_shapes=[pltpu.VMEM((tm + HALO, tn), F32)],
        compiler_params=_cparams(2, vmem),
        name="ffn_up",
    )(h, w_up, w_up, conv_w, conv_b.reshape(1, d_ff))


def _mm_gdn_proj_kernel(*refs, kinds, tiles_per_seq, sub):
    n = len(kinds)
    n_conv = sum(kind[0] != "plain" for kind in kinds)
    a_ref = refs[0]
    w_refs = refs[1:1 + n]
    cw_refs = iter(refs[1 + n:1 + n + n_conv])
    o_refs = refs[1 + n + n_conv:1 + 2 * n + n_conv]
    ybuf_refs = iter(refs[1 + 2 * n + n_conv:])
    conv_refs = [None if kind[0] == "plain" else (next(cw_refs), next(ybuf_refs)) for kind in kinds]
    tm = a_ref.shape[0]
    for conv in conv_refs:
        if conv is not None:
            _conv_begin(conv[1], tm, tiles_per_seq)
    for r0 in range(0, tm, sub):
        a = a_ref[r0:r0 + sub, :]
        for kind, w_ref, conv, o_ref in zip(kinds, w_refs, conv_refs, o_refs):
            y = _dot(a, w_ref[...])
            if conv is not None:
                y = _silu(_conv_rows(y, conv[1], conv[0], r0))
            if kind[0] != "l2":
                o_ref[r0:r0 + sub, :] = y.astype(o_ref.dtype)
                continue
            for g in range(y.shape[1] // HEAD_DIM):
                cols = slice(g * HEAD_DIM, (g + 1) * HEAD_DIM)
                yg = y[:, cols]
                inv = lax.rsqrt(jnp.sum(yg * yg, axis=-1, keepdims=True) + EPS)
                o_ref[r0:r0 + sub, cols] = (yg * (inv * kind[1])).astype(o_ref.dtype)


def _gdn_proj(h, w_in, layer, conv_w, streams, n_out, seq, tm, tn, name):
    m, k = h.shape
    sub = _tile(tm, GDN_SUB_ROWS, 8)
    n = len(streams)
    conv_streams = [s for s in streams if s[1][0] != "plain"]
    vmem = n * (2 * k * tn * 2 + 3 * tm * tn * 4 + 8 * sub * tn * 4) + 2 * tm * k * 2
    w_specs = [pl.BlockSpec((None, k, tn), functools.partial(lambda j, i, c0: (layer, 0, j + c0), c0=s[0]))
               for s in streams]
    cw_specs = [pl.BlockSpec((conv_w.shape[0], tn), functools.partial(lambda j, i, c0: (0, j + c0), c0=s[0]))
                for s in conv_streams]
    out_spec = pl.BlockSpec((tm, tn), lambda j, i: (i, j))
    return pl.pallas_call(
        functools.partial(_mm_gdn_proj_kernel, kinds=tuple(s[1] for s in streams),
                          tiles_per_seq=seq // tm, sub=sub),
        grid=(n_out // tn, m // tm),
        in_specs=[pl.BlockSpec((tm, k), lambda j, i: (i, 0))] + w_specs + cw_specs,
        out_specs=[out_spec] * n,
        out_shape=[jax.ShapeDtypeStruct((m, n_out), s[2]) for s in streams],
        scratch_shapes=[pltpu.VMEM((tm + HALO, tn), F32) for _ in conv_streams],
        compiler_params=_cparams(2, vmem),
        name=name,
    )(h, *([w_in] * n), *([conv_w] * len(conv_streams)))


def _chunk_masks(blk, ch):
    shift = ch.bit_length() - 1
    r = lax.broadcasted_iota(jnp.int32, (blk, blk), 0)
    c = lax.broadcasted_iota(jnp.int32, (blk, blk), 1)
    same_chunk = jnp.right_shift(r, shift) == jnp.right_shift(c, shift)
    return (jnp.logical_and(same_chunk, r >= c), jnp.logical_and(same_chunk, r > c), same_chunk)


def _mm_gates_kernel(a_ref, w_ref, alog_ref, dt_ref, o_ref, *, n_heads):
    tm = a_ref.shape[0]
    blk = GDN_BLOCK
    lower_incl, _, same_chunk = _chunk_masks(blk, GDN_CHUNK)
    cum_mat = lower_incl.astype(BF16)
    tot_mat = same_chunk.astype(BF16)
    lane = lax.broadcasted_iota(jnp.int32, (blk, HEAD_DIM), 1)
    w = w_ref[...]
    for r0 in range(0, tm, blk):
        y = _dot(a_ref[r0:r0 + blk, :], w)
        g = -jnp.exp(alog_ref[...]) * _softplus(y + dt_ref[...])
        hi, mid, lo = _split3(g)
        cum = _dot(cum_mat, hi) + _dot(cum_mat, mid) + _dot(cum_mat, lo)
        tot = _dot(tot_mat, hi) + _dot(tot_mat, mid) + _dot(tot_mat, lo)
        o_ref[r0:r0 + blk, :] = jnp.where(lane < n_heads, cum,
                                          jnp.where(lane < 2 * n_heads, _sigmoid(y), tot))


def _gdn_gates(h, w_ab, a_log, dt_bias, tm):
    m, k = h.shape
    n_heads = a_log.shape[0]
    tn = HEAD_DIM
    assert 3 * n_heads <= tn and tm % GDN_BLOCK == 0
    w_a, w_b = w_ab[:, :n_heads], w_ab[:, n_heads:]
    w_pad = jnp.concatenate([w_a, w_b, w_a, jnp.zeros((k, tn - 3 * n_heads), F32)], axis=1).astype(BF16)
    zeros_h = jnp.zeros((n_heads,), F32)
    pad = lambda v: jnp.concatenate([v, zeros_h, v, jnp.zeros((tn - 3 * n_heads,), F32)]).reshape(1, tn)
    vmem = 2 * k * tn * 2 + 2 * tm * k * 2 + 4 * tm * tn * 4
    return pl.pallas_call(
        functools.partial(_mm_gates_kernel, n_heads=n_heads),
        grid=(m // tm,),
        in_specs=[pl.BlockSpec((tm, k), lambda i: (i, 0)),
                  pl.BlockSpec((k, tn), lambda i: (0, 0)),
                  pl.BlockSpec((1, tn), lambda i: (0, 0)),
                  pl.BlockSpec((1, tn), lambda i: (0, 0))],
        out_specs=pl.BlockSpec((tm, tn), lambda i: (i, 0)),
        out_shape=jax.ShapeDtypeStruct((m, tn), F32),
        compiler_params=_cparams(1, vmem),
        name="gdn_gates",
    )(h, w_pad, pad(a_log), pad(dt_bias))


def _sb_kernel(q_ref, k_ref, v_ref, o_ref, *, tq, n_pack, scale):
    qi = pl.program_id(2)
    row = lax.broadcasted_iota(jnp.int32, (tq, tq), 0)
    col = lax.broadcasted_iota(jnp.int32, (tq, tq), 1)
    later = (row > col).astype(BF16)
    below_diag = col < row
    head_cols = [slice(p * HEAD_DIM, (p + 1) * HEAD_DIM) for p in range(n_pack)]
    qs = [q_ref[:, cols] for cols in head_cols]

    def body(carry):
        kb, cs, accs, _ = carry
        start = pl.multiple_of(kb * tq, tq)
        strict = jnp.logical_or(kb < qi, below_diag)
        heads = range(n_pack)
        z = [_dot_nt(qs[p], k_ref[pl.ds(start, tq), head_cols[p]]) * scale for p in heads]
        sp = [jnp.maximum(z[p], 0.0) + jnp.log(1.0 + jnp.exp(-jnp.abs(z[p]))) for p in heads]
        log_1m = [jnp.where(strict, -sp[p], 0.0) for p in heads]
        tail = [_dot(log_1m[p].astype(BF16), later) + cs[p] for p in heads]
        w = [jnp.where(strict, jnp.exp(z[p] - sp[p] + tail[p]), 0.0).astype(BF16) for p in heads]
        new_accs = [accs[p] + _dot(w[p], v_ref[pl.ds(start, tq), head_cols[p]]) for p in heads]
        new_cs = [cs[p] + jnp.sum(log_1m[p], axis=1, keepdims=True) for p in heads]
        c_max = functools.reduce(jnp.maximum, [jnp.max(c) for c in new_cs])
        return kb - 1, tuple(new_cs), tuple(new_accs), c_max > -F32_EXP_UNDERFLOW

    def cond(carry):
        kb, _, _, live = carry
        return jnp.logical_and(kb >= 0, live)

    init = (qi, tuple(jnp.zeros((tq, 1), F32) for _ in range(n_pack)),
            tuple(jnp.zeros((tq, HEAD_DIM), F32) for _ in range(n_pack)), jnp.bool_(True))
    _, _, accs, _ = lax.while_loop(cond, body, init)
    for cols, acc in zip(head_cols, accs):
        o_ref[:, cols] = acc.astype(o_ref.dtype)


def _stick_breaking(proj, bsz, seq, n_heads, tq):
    t = proj.shape[0]
    nq = seq // tq
    n_pack = SB_HEAD_PACK if n_heads % SB_HEAD_PACK == 0 else 1
    groups = n_heads // n_pack
    width = n_pack * HEAD_DIM
    vmem = 2 * 2 * seq * width * 2 + 4 * tq * width * 2 + n_pack * 12 * tq * tq * 4
    return pl.pallas_call(
        functools.partial(_sb_kernel, tq=tq, n_pack=n_pack, scale=HEAD_DIM ** -0.5),
        grid=(bsz, groups, nq),
        in_specs=[pl.BlockSpec((tq, width), lambda b, h, i: (b * nq + i, h)),
                  pl.BlockSpec((seq, width), lambda b, h, i: (b, groups + h)),
                  pl.BlockSpec((seq, width), lambda b, h, i: (b, 2 * groups + h))],
        out_specs=pl.BlockSpec((tq, width), lambda b, h, i: (b * nq + i, h)),
        out_shape=jax.ShapeDtypeStruct((t, n_heads * HEAD_DIM), BF16),
        compiler_params=_cparams(3, vmem),
        name="stick_breaking",
    )(proj, proj, proj)


def _sgu_kernel(u_ref, v_ref, gain_ref, w_ref, b_ref, o_ref):
    rows = u_ref.shape[0]
    r = lax.broadcasted_iota(jnp.int32, (SGU_LEN, SGU_LEN), 0)
    c = lax.broadcasted_iota(jnp.int32, (SGU_LEN, SGU_LEN), 1)
    w = jnp.where(r >= c, w_ref[0], 0.0).astype(BF16)
    bias = b_ref[0]
    gain = gain_ref[0]
    for n in range(rows // SGU_LEN):
        sl = slice(n * SGU_LEN, (n + 1) * SGU_LEN)
        u = _gelu(u_ref[sl, :].astype(F32))
        v = _gelu(v_ref[sl, :].astype(F32))
        v = v * lax.rsqrt(jnp.mean(v * v, axis=-1, keepdims=True) + EPS) * gain
        mixed = _dot(w, v.astype(BF16)) + bias
        o_ref[sl, :] = (u * mixed).astype(o_ref.dtype)


def _spatial_gating(proj, sgu_gain, sgu_w, sgu_b, n_groups, u_block0, rows):
    t = proj.shape[0]
    return pl.pallas_call(
        _sgu_kernel,
        grid=(n_groups, t // rows),
        in_specs=[pl.BlockSpec((rows, HEAD_DIM), lambda g, i: (i, u_block0 + g)),
                  pl.BlockSpec((rows, HEAD_DIM), lambda g, i: (i, u_block0 + n_groups + g)),
                  pl.BlockSpec((1, 1, HEAD_DIM), lambda g, i: (g, 0, 0)),
                  pl.BlockSpec((1, SGU_LEN, SGU_LEN), lambda g, i: (g, 0, 0)),
                  pl.BlockSpec((1, SGU_LEN, 1), lambda g, i: (g, 0, 0))],
        out_specs=pl.BlockSpec((rows, HEAD_DIM), lambda g, i: (i, g)),
        out_shape=jax.ShapeDtypeStruct((t, n_groups * HEAD_DIM), BF16),
        compiler_params=_cparams(2, 16 * rows * HEAD_DIM * 4),
        name="spatial_gating",
    )(proj, proj, sgu_gain.reshape(n_groups, 1, HEAD_DIM), sgu_w,
      sgu_b.reshape(n_groups, SGU_LEN, 1))


def _neumann_solve(p_mats, xs, n_factors):
    idx = range(len(xs))
    n = p_mats[0].shape[0]
    eye = (lax.broadcasted_iota(jnp.int32, (n, n), 0)
           == lax.broadcasted_iota(jnp.int32, (n, n), 1)).astype(F32)
    ts = [eye + p_mats[i] for i in idx]
    for _ in range(1, n_factors):
        pb = [p_mats[i].astype(BF16) for i in idx]
        p_mats = [_dot(pb[i], pb[i]) for i in idx]
        ts = [ts[i] + _dot(p_mats[i].astype(BF16), ts[i].astype(BF16)) for i in idx]
    return [_dot(ts[i].astype(BF16), xs[i].astype(BF16)) for i in idx]


def _gdn_kernel(q_ref, k_ref, v_ref, z_ref, gb_ref, gain_ref, o_ref, state_ref, *, n_heads, n_pack):
    hg = pl.program_id(1)
    n_rows = q_ref.shape[0]
    blk, ch = GDN_BLOCK, GDN_CHUNK
    n_ch = blk // ch
    n_factors = ch.bit_length() - 1

    @pl.when(pl.program_id(2) == 0)
    def _():
        state_ref[...] = jnp.zeros(state_ref.shape, F32)

    lower_incl, lower_strict, _ = _chunk_masks(blk, ch)
    lane = lax.broadcasted_iota(jnp.int32, (blk, HEAD_DIM), 1)
    gain = gain_ref[...]

    def pick(gb, lane_idx):
        return jnp.sum(jnp.where(lane == lane_idx, gb, 0.0), axis=1, keepdims=True)

    def diff_operands(gc):
        chi, cmid, clo = (part.astype(F32) for part in _split3(jnp.broadcast_to(gc, (blk, HEAD_DIM))))
        left = jnp.where(lane == 0, chi, jnp.where(lane == 1, cmid, jnp.where(
            lane == 2, clo, jnp.where(lane < 6, 1.0, 0.0))))
        right = jnp.where(lane == 3, -chi, jnp.where(lane == 4, -cmid, jnp.where(
            lane == 5, -clo, jnp.where(lane < 3, 1.0, 0.0))))
        return left.astype(BF16), right.astype(BF16)

    head_cols = [slice(p * HEAD_DIM, (p + 1) * HEAD_DIM) for p in range(n_pack)]
    heads = range(n_pack)

    def block(b, states):
        rows = pl.ds(pl.multiple_of(b * blk, blk), blk)
        gb = gb_ref[rows, :]
        q = [q_ref[rows, cols].astype(F32) for cols in head_cols]
        k = [k_ref[rows, cols].astype(F32) for cols in head_cols]
        v = [v_ref[rows, cols].astype(F32) for cols in head_cols]
        gc = [pick(gb, hg * n_pack + p) for p in heads]
        beta = [pick(gb, hg * n_pack + p + n_heads) for p in heads]
        gl = [pick(gb, hg * n_pack + p + 2 * n_heads) for p in heads]
        diff = [_dot_nt(*diff_operands(g)) for g in gc]
        decay = [jnp.exp(jnp.where(lower_incl, d, -1e30)) for d in diff]
        eg = [jnp.exp(g) for g in gc]
        kb = [k[p] * beta[p] for p in heads]
        kbf = [k[p].astype(BF16) for p in heads]
        p_mat = [jnp.where(lower_strict, _dot_nt((-kb[p]).astype(BF16), kbf[p]) * decay[p], 0.0)
                 for p in heads]
        qk = [(_dot_nt(q[p].astype(BF16), kbf[p]) * decay[p]).astype(BF16) for p in heads]
        x = [jnp.concatenate([v[p] * beta[p], kb[p] * eg[p]], axis=1) for p in heads]
        x = _neumann_solve(p_mat, x, n_factors)
        u = [x[p][:, :HEAD_DIM] for p in heads]
        w = [x[p][:, HEAD_DIM:].astype(BF16) for p in heads]
        q_dec = [(q[p] * eg[p]).astype(BF16) for p in heads]
        k_dec = [(k[p] * jnp.exp(gl[p] - gc[p])).astype(BF16) for p in heads]
        g_tot = [jnp.exp(gl[p]) for p in heads]

        states = list(states)
        o_inter = [[] for _ in heads]
        v_new = [[] for _ in heads]
        for ci in range(n_ch):
            sl = slice(ci * ch, (ci + 1) * ch)
            sb = [states[p].astype(BF16) for p in heads]
            v_c = [u[p][sl] - _dot(w[p][sl], sb[p]) for p in heads]
            for p in heads:
                o_inter[p].append(_dot(q_dec[p][sl], sb[p]))
                v_new[p].append(v_c[p])
            states = [states[p] * g_tot[p][ci * ch:ci * ch + 1, :]
                      + _dot_tn(k_dec[p][sl], v_c[p].astype(BF16)) for p in heads]
        o = [jnp.concatenate(o_inter[p], axis=0)
             + _dot(qk[p], jnp.concatenate(v_new[p], axis=0).astype(BF16)) for p in heads]
        for p, cols in enumerate(head_cols):
            on = o[p] * lax.rsqrt(jnp.mean(o[p] * o[p], axis=-1, keepdims=True) + EPS) * gain
            o_ref[rows, cols] = (on * _silu(z_ref[rows, cols].astype(F32))).astype(o_ref.dtype)
        return tuple(states)

    states = lax.fori_loop(0, n_rows // blk, block, tuple(state_ref[p] for p in range(n_pack)))
    for p in range(n_pack):
        state_ref[p] = states[p]


def _gated_delta(q, k, v, z, gates, o_gain, bsz, seq, n_heads):
    t = q.shape[0]
    n_pack = GDN_HEAD_PACK if n_heads % GDN_HEAD_PACK == 0 else 1
    width = n_pack * HEAD_DIM
    rows = _tile(seq, 2048, GDN_BLOCK)
    tiles = seq // rows
    head_spec = pl.BlockSpec((rows, width), lambda b, h, r: (b * tiles + r, h))
    io_bytes = sum(jnp.dtype(a.dtype).itemsize for a in (q, k, v, z)) + 2
    vmem = 2 * (rows * width * io_bytes + rows * HEAD_DIM * 4) + n_pack * 40 * GDN_BLOCK * GDN_BLOCK * 4
    return pl.pallas_call(
        functools.partial(_gdn_kernel, n_heads=n_heads, n_pack=n_pack),
        grid=(bsz, n_heads // n_pack, tiles),
        in_specs=[head_spec, head_spec, head_spec, head_spec,
                  pl.BlockSpec((rows, HEAD_DIM), lambda b, h, r: (b * tiles + r, 0)),
                  pl.BlockSpec((1, HEAD_DIM), lambda b, h, r: (0, 0))],
        out_specs=head_spec,
        out_shape=jax.ShapeDtypeStruct((t, n_heads * HEAD_DIM), BF16),
        scratch_shapes=[pltpu.VMEM((n_pack, HEAD_DIM, HEAD_DIM), F32)],
        compiler_params=_cparams(3, vmem),
        name="gated_delta",
    )(q, k, v, z, gates, o_gain.reshape(1, HEAD_DIM))


def kernel(x, c, ada_w, ada_b, ada_layer, norm_mix, norm_ffn, norm_final, ev_w_in, ev_w_out, sgu_gain, sgu_w, sgu_b, gdn_w_in, gdn_conv, gdn_a_log, gdn_dt_bias, gdn_o_gain, gdn_w_out, ffn_w_up, ffn_conv, ffn_conv_b, ffn_w_down):
    bsz, seq, d = x.shape
    depth = ada_layer.shape[0]
    t = bsz * seq
    d_ff = ffn_w_down.shape[1]
    n_sb = d // 2 // HEAD_DIM
    n_sgu = sgu_gain.shape[1]
    n_gdn = gdn_a_log.shape[1]
    assert sgu_gain.shape[2] == HEAD_DIM and d == n_gdn * HEAD_DIM and sgu_w.shape[2] == SGU_LEN
    assert n_sgu == n_sb and seq % GDN_BLOCK == 0
    tm = _tile(seq, 1024)

    ev_w_in_b, ev_w_out_b = _to_bf16(ev_w_in), _to_bf16(ev_w_out)
    gdn_w_in_b, gdn_w_out_b = _to_bf16(gdn_w_in), _to_bf16(gdn_w_out)
    ffn_w_up_b, ffn_w_down_b = _to_bf16(ffn_w_up), _to_bf16(ffn_w_down)
    gdn_w_ab = gdn_w_in[:, :, 4 * d:]

    mod_rows = _ada_project(c, ada_w, ada_b).reshape(bsz * N_MOD, 1, d)
    layer_rows = ada_layer.reshape(depth * N_MOD, 1, d)
    x = x.reshape(t, d)

    for layer in range(depth):
        h = _norm_modulate(x, norm_mix[layer], mod_rows, layer_rows, layer, 0, seq)
        if layer % 2 == 0:
            e = layer // 2
            n_proj = 5 * (d // 2)
            proj = _matmul(h, ev_w_in_b, e, 0, n_proj, BF16, tm, _tile(n_proj, 1024), "even_in_proj")
            o_a = _stick_breaking(proj, bsz, seq, n_sb, _tile(seq, 256))
            o_b = _spatial_gating(proj, sgu_gain[e], sgu_w[e], sgu_b[e], n_sgu, 3 * n_sb,
                                  _tile(seq, 2048))
            x = _matmul_residual([o_a, o_b], ev_w_out_b, e, x, mod_rows, layer_rows, layer, 2, seq,
                                 tm, _tile(d, 512), "even_out_proj")
        else:
            o = layer // 2
            tn = _tile(d, 512)
            nb = d // tn
            conv_w = gdn_conv[o]
            q, k = _gdn_proj(h, gdn_w_in_b, o, conv_w,
                             [(0, ("l2", HEAD_DIM ** -0.5), BF16), (nb, ("l2", 1.0), BF16)],
                             d, seq, tm, tn, "gdn_qk_proj")
            v, z = _gdn_proj(h, gdn_w_in_b, o, conv_w,
                             [(2 * nb, ("silu",), F32), (3 * nb, ("plain",), F32)],
                             d, seq, tm, tn, "gdn_vz_proj")
            gates = _gdn_gates(h, gdn_w_ab[o], gdn_a_log[o], gdn_dt_bias[o], tm)
            y = _gated_delta(q, k, v, z, gates, gdn_o_gain[o], bsz, seq, n_gdn)
            x = _matmul_residual([y], gdn_w_out_b, o, x, mod_rows, layer_rows, layer, 2, seq,
                                 tm, _tile(d, 512), "gdn_out_proj")
        h = _norm_modulate(x, norm_ffn[layer], mod_rows, layer_rows, layer, 3, seq)
        f = _ffn_up(h, ffn_w_up_b, layer, ffn_conv[layer], ffn_conv_b[layer], seq, tm,
                    _tile(d_ff, 512))
        x = _matmul_residual([f], ffn_w_down_b, layer, x, mod_rows, layer_rows, layer, 5, seq,
                             _tile(seq, 512), _tile(d, 512), "ffn_down_proj")
    return _final_norm(x, norm_final).reshape(bsz, seq, d)
```

```python
import functools
import math

import jax
import jax.numpy as jnp
from jax import lax
from jax.experimental import pallas as pl
from jax.experimental.pallas import tpu as pltpu

F32 = jnp.float32
BF16 = jnp.bfloat16

HEAD_DIM = 128
SGU_LEN = 128
GDN_CHUNK = 64
GDN_BLOCK = 256
SB_HEAD_PACK = 2
GDN_HEAD_PACK = 4
N_MOD = 6
EPS = 1e-6
F32_EXP_UNDERFLOW = 104.0
V7X_VMEM_BYTES = 64 * 1024 * 1024
VMEM_CAP = V7X_VMEM_BYTES - 6 * 1024 * 1024
MIB = 1024 * 1024


def _cparams(n_axes, vmem_bytes):
    limit = int(min(VMEM_CAP, max(32 * MIB, vmem_bytes * 5 // 4 + 4 * MIB)))
    return pltpu.CompilerParams(dimension_semantics=("arbitrary",) * n_axes,
                                vmem_limit_bytes=limit)


def _tile(n, pref, unit=128):
    if n <= pref:
        return n
    t = pref // unit * unit
    while n % t:
        t -= unit
    assert t > 0, (n, pref)
    return t


def _dot(a, b):
    return jnp.dot(a, b, preferred_element_type=F32)


def _dot_nt(a, b):
    return lax.dot_general(a, b, (((1,), (1,)), ((), ())), preferred_element_type=F32)


def _dot_tn(a, b):
    return lax.dot_general(a, b, (((0,), (0,)), ((), ())), preferred_element_type=F32)


def _split3(x):
    hi = x.astype(BF16)
    r = x - hi.astype(F32)
    mid = r.astype(BF16)
    lo = (r - mid.astype(F32)).astype(BF16)
    return hi, mid, lo


def _softplus(x):
    return jnp.maximum(x, 0.0) + jnp.log1p(jnp.exp(-jnp.abs(x)))


def _sigmoid(x):
    return 1.0 / (1.0 + jnp.exp(-x))


def _silu(x):
    return x * _sigmoid(x)


def _gelu(x):
    c = math.sqrt(2.0 / math.pi)
    return x * (0.5 * (1.0 + jnp.tanh(c * (x + 0.044715 * (x * x * x)))))


def _cast_kernel(w_ref, o_ref):
    o_ref[...] = w_ref[...].astype(o_ref.dtype)


def _to_bf16(w, layer):
    _, rows, cols = w.shape
    tr = _tile(rows, max(16, 4 * MIB // (cols * 4)), unit=16)
    return pl.pallas_call(
        _cast_kernel,
        grid=(rows // tr,),
        in_specs=[pl.BlockSpec((None, tr, cols), lambda i: (layer, i, 0))],
        out_specs=pl.BlockSpec((tr, cols), lambda i: (i, 0)),
        out_shape=jax.ShapeDtypeStruct((rows, cols), BF16),
        compiler_params=_cparams(1, 2 * tr * cols * 6),
        name="weights_to_bf16",
    )(w)


def _side_cast_plan(side, n_inner, n_steps):
    in_specs, out_specs, out_shapes, vmem = [], [], [], 0
    for w, layer in side:
        _, rows, cols = w.shape
        tr = 16
        while rows % tr or rows // tr > n_steps:
            tr += 16
        last = rows // tr - 1
        idx = functools.partial(lambda j, i, last: (jnp.minimum(j * n_inner + i, last), 0), last=last)
        in_specs.append(pl.BlockSpec(
            (None, tr, cols), functools.partial(lambda j, i, layer, idx: (layer,) + idx(j, i), layer=layer, idx=idx)))
        out_specs.append(pl.BlockSpec((tr, cols), idx))
        out_shapes.append(jax.ShapeDtypeStruct((rows, cols), BF16))
        vmem += 2 * tr * cols * 6
    return in_specs, out_specs, out_shapes, vmem


def _side_cast(in_refs, out_refs):
    for w_ref, o_ref in zip(in_refs, out_refs):
        o_ref[...] = w_ref[...].astype(o_ref.dtype)


def _ada_kernel(c_ref, w_ref, b_ref, o_ref):
    c = c_ref[...]
    o_ref[...] = _dot(_silu(c).astype(BF16), w_ref[...].astype(BF16)) + b_ref[...]


def _ada_project(c, ada_w, ada_b):
    bsz, d = c.shape
    n = ada_w.shape[1]
    rows = 8
    tn = _tile(n, 512)
    c_pad = jnp.zeros((rows, d), F32).at[:bsz].set(c)
    out = pl.pallas_call(
        _ada_kernel,
        grid=(n // tn,),
        in_specs=[pl.BlockSpec((rows, d), lambda j: (0, 0)),
                  pl.BlockSpec((d, tn), lambda j: (0, j)),
                  pl.BlockSpec((1, tn), lambda j: (0, j))],
        out_specs=pl.BlockSpec((rows, tn), lambda j: (0, j)),
        out_shape=jax.ShapeDtypeStruct((rows, n), F32),
        compiler_params=_cparams(1, 2 * d * tn * 4 + d * tn * 2),
        name="ada_project",
    )(c_pad, ada_w, ada_b.reshape(1, n))
    return out[:bsz]


def _norm_mod_kernel(x_ref, g_ref, sh_ref, sc_ref, lsh_ref, lsc_ref, o_ref):
    x = x_ref[...]
    y = x * lax.rsqrt(jnp.mean(x * x, axis=-1, keepdims=True) + EPS) * g_ref[...]
    scale = 1.0 + (sc_ref[0] + lsc_ref[0])
    shift = sh_ref[0] + lsh_ref[0]
    o_ref[...] = (y * scale + shift).astype(o_ref.dtype)


def _norm_modulate(x, gain, mod_rows, layer_rows, layer, shift_idx, seq):
    t, d = x.shape
    tr = _tile(seq, 512)
    per_seq = seq // tr
    mod_spec = lambda idx: pl.BlockSpec((1, 1, d), lambda i: ((i // per_seq) * N_MOD + idx, 0, 0))
    lay_spec = lambda idx: pl.BlockSpec((1, 1, d), lambda i: (layer * N_MOD + idx, 0, 0))
    return pl.pallas_call(
        _norm_mod_kernel,
        grid=(t // tr,),
        in_specs=[pl.BlockSpec((tr, d), lambda i: (i, 0)),
                  pl.BlockSpec((1, d), lambda i: (0, 0)),
                  mod_spec(shift_idx), mod_spec(shift_idx + 1),
                  lay_spec(shift_idx), lay_spec(shift_idx + 1)],
        out_specs=pl.BlockSpec((tr, d), lambda i: (i, 0)),
        out_shape=jax.ShapeDtypeStruct((t, d), BF16),
        compiler_params=_cparams(1, 2 * tr * d * 6),
        name="norm_modulate",
    )(x, gain.reshape(1, d), mod_rows, mod_rows, layer_rows, layer_rows)


def _final_norm_kernel(x_ref, g_ref, o_ref):
    x = x_ref[...]
    o_ref[...] = x * lax.rsqrt(jnp.mean(x * x, axis=-1, keepdims=True) + EPS) * g_ref[...]


def _final_norm(x, gain):
    t, d = x.shape
    tr = _tile(t, 256)
    return pl.pallas_call(
        _final_norm_kernel,
        grid=(t // tr,),
        in_specs=[pl.BlockSpec((tr, d), lambda i: (i, 0)),
                  pl.BlockSpec((1, d), lambda i: (0, 0))],
        out_specs=pl.BlockSpec((tr, d), lambda i: (i, 0)),
        out_shape=jax.ShapeDtypeStruct((t, d), F32),
        compiler_params=_cparams(1, 2 * tr * d * 8),
        name="final_norm",
    )(x, gain.reshape(1, d))


def _mm_plain_kernel(a_ref, w_ref, o_ref):
    o_ref[...] = _dot(a_ref[...], w_ref[...]).astype(o_ref.dtype)


def _matmul(a, w, col_block0, n_out, out_dtype, tm, tn, name):
    m, k = a.shape
    out_bytes = jnp.dtype(out_dtype).itemsize
    vmem = 2 * k * tn * 2 + 2 * tm * k * 2 + 2 * tm * tn * out_bytes + tm * tn * 4
    return pl.pallas_call(
        _mm_plain_kernel,
        grid=(n_out // tn, m // tm),
        in_specs=[pl.BlockSpec((tm, k), lambda j, i: (i, 0)),
                  pl.BlockSpec((k, tn), lambda j, i: (0, j + col_block0))],
        out_specs=pl.BlockSpec((tm, tn), lambda j, i: (i, j)),
        out_shape=jax.ShapeDtypeStruct((m, n_out), out_dtype),
        compiler_params=_cparams(2, vmem),
        name=name,
    )(a, w)


def _mm_resid_kernel(*refs, n_parts, n_side):
    a_refs = refs[:n_parts]
    w_refs = refs[n_parts:2 * n_parts]
    x_ref, gate_ref, lgate_ref = refs[2 * n_parts:2 * n_parts + 3]
    side_in = refs[2 * n_parts + 3:2 * n_parts + 3 + n_side]
    o_ref = refs[2 * n_parts + 3 + n_side]
    side_out = refs[2 * n_parts + 4 + n_side:]
    _side_cast(side_in, side_out)
    y = _dot(a_refs[0][...], w_refs[0][...])
    for a_ref, w_ref in zip(a_refs[1:], w_refs[1:]):
        y = y + _dot(a_ref[...], w_ref[...])
    o_ref[...] = x_ref[...] + (gate_ref[0] + lgate_ref[0]) * y


def _matmul_residual(a_parts, w, x, mod_rows, layer_rows, layer, gate_idx, seq, tm, tn, name, side=()):
    n_parts = len(a_parts)
    m, kp = a_parts[0].shape
    n = w.shape[1]
    per_seq = seq // tm
    grid = (n // tn, m // tm)
    side_in, side_out, side_shapes, side_vmem = _side_cast_plan(side, grid[1], grid[0] * grid[1])
    vmem = n_parts * (2 * kp * tn * 2 + 2 * tm * kp * 2) + 5 * tm * tn * 4 + side_vmem
    a_specs = [pl.BlockSpec((tm, kp), lambda j, i: (i, 0)) for _ in range(n_parts)]
    w_specs = [pl.BlockSpec((kp, tn), functools.partial(lambda j, i, p: (p, j), p=p))
               for p in range(n_parts)]
    out, *cast = pl.pallas_call(
        functools.partial(_mm_resid_kernel, n_parts=n_parts, n_side=len(side)),
        grid=grid,
        in_specs=a_specs + w_specs + [
            pl.BlockSpec((tm, tn), lambda j, i: (i, j)),
            pl.BlockSpec((1, 1, tn), lambda j, i: ((i // per_seq) * N_MOD + gate_idx, 0, j)),
            pl.BlockSpec((1, 1, tn), lambda j, i: (layer * N_MOD + gate_idx, 0, j))] + side_in,
        out_specs=[pl.BlockSpec((tm, tn), lambda j, i: (i, j))] + side_out,
        out_shape=[jax.ShapeDtypeStruct((m, n), F32)] + side_shapes,
        compiler_params=_cparams(2, vmem),
        name=name,
    )(*a_parts, *([w] * n_parts), x, mod_rows, layer_rows, *[w_side for w_side, _ in side])
    return out, cast


HALO = 8
FFN_SUB_ROWS = 256
GDN_SUB_ROWS = 128


def _conv_begin(ybuf_ref, tm, tiles_per_seq):
    starts_sequence = (pl.program_id(1) % tiles_per_seq) == 0

    @pl.when(starts_sequence)
    def _():
        ybuf_ref[0:HALO, :] = jnp.zeros((HALO, ybuf_ref.shape[1]), F32)

    @pl.when(jnp.logical_not(starts_sequence))
    def _():
        ybuf_ref[0:HALO, :] = ybuf_ref[tm:tm + HALO, :]


def _conv_rows(y, ybuf_ref, cw_ref, r0):
    rows = y.shape[0]
    taps = cw_ref.shape[0]
    ybuf_ref[HALO + r0:HALO + r0 + rows, :] = y
    out = cw_ref[taps - 1:taps, :] * y
    for j in range(taps - 1):
        off = HALO + r0 - (taps - 1) + j
        out = out + cw_ref[j:j + 1, :] * ybuf_ref[off:off + rows, :]
    return out


def _mm_ffn_up_kernel(*refs, tiles_per_seq, sub, n_side):
    a_ref, wg_ref, wv_ref, cw_ref, cb_ref = refs[:5]
    side_in = refs[5:5 + n_side]
    o_ref = refs[5 + n_side]
    side_out = refs[6 + n_side:6 + 2 * n_side]
    ybuf_ref = refs[6 + 2 * n_side]
    _side_cast(side_in, side_out)
    tm = a_ref.shape[0]
    _conv_begin(ybuf_ref, tm, tiles_per_seq)
    for r0 in range(0, tm, sub):
        a = a_ref[r0:r0 + sub, :]
        gate = _conv_rows(_dot(a, wg_ref[...]), ybuf_ref, cw_ref, r0) + cb_ref[...]
        o_ref[r0:r0 + sub, :] = (_gelu(gate) * _dot(a, wv_ref[...])).astype(o_ref.dtype)


def _ffn_up(h, w_up, conv_w, conv_b, seq, tm, tn, side=()):
    m, k = h.shape
    d_ff = w_up.shape[1] // 2
    nb = d_ff // tn
    sub = _tile(tm, FFN_SUB_ROWS, 8)
    grid = (nb, m // tm)
    side_in, side_out, side_shapes, side_vmem = _side_cast_plan(side, grid[1], grid[0] * grid[1])
    vmem = 2 * 2 * k * tn * 2 + 2 * tm * k * 2 + 3 * tm * tn * 4 + 8 * sub * tn * 4 + side_vmem
    out, *cast = pl.pallas_call(
        functools.partial(_mm_ffn_up_kernel, tiles_per_seq=seq // tm, sub=sub, n_side=len(side)),
        grid=grid,
        in_specs=[pl.BlockSpec((tm, k), lambda j, i: (i, 0)),
                  pl.BlockSpec((k, tn), lambda j, i: (0, j)),
                  pl.BlockSpec((k, tn), lambda j, i: (0, j + nb)),
                  pl.BlockSpec((conv_w.shape[0], tn), lambda j, i: (0, j)),
                  pl.BlockSpec((1, tn), lambda j, i: (0, j))] + side_in,
        out_specs=[pl.BlockSpec((tm, tn), lambda j, i: (i, j))] + side_out,
        out_shape=[jax.ShapeDtypeStruct((m, d_ff), BF16)] + side_shapes,
        scratch_shapes=[pltpu.VMEM((tm + HALO, tn), F32)],
        compiler_params=_cparams(2, vmem),
        name="ffn_up",
    )(h, w_up, w_up, conv_w, conv_b.reshape(1, d_ff), *[w_side for w_side, _ in side])
    return out, cast


def _mm_gdn_proj_kernel(*refs, kinds, tiles_per_seq, sub):
    n = len(kinds)
    n_conv = sum(kind[0] != "plain" for kind in kinds)
    a_ref = refs[0]
    w_refs = refs[1:1 + n]
    cw_refs = iter(refs[1 + n:1 + n + n_conv])
    o_refs = refs[1 + n + n_conv:1 + 2 * n + n_conv]
    ybuf_refs = iter(refs[1 + 2 * n + n_conv:])
    conv_refs = [None if kind[0] == "plain" else (next(cw_refs), next(ybuf_refs)) for kind in kinds]
    tm = a_ref.shape[0]
    for conv in conv_refs:
        if conv is not None:
            _conv_begin(conv[1], tm, tiles_per_seq)
    for r0 in range(0, tm, sub):
        a = a_ref[r0:r0 + sub, :]
        for kind, w_ref, conv, o_ref in zip(kinds, w_refs, conv_refs, o_refs):
            y = _dot(a, w_ref[...])
            if conv is not None:
                y = _silu(_conv_rows(y, conv[1], conv[0], r0))
            if kind[0] != "l2":
                o_ref[r0:r0 + sub, :] = y.astype(o_ref.dtype)
                continue
            for g in range(y.shape[1] // HEAD_DIM):
                cols = slice(g * HEAD_DIM, (g + 1) * HEAD_DIM)
                yg = y[:, cols]
                inv = lax.rsqrt(jnp.sum(yg * yg, axis=-1, keepdims=True) + EPS)
                o_ref[r0:r0 + sub, cols] = (yg * (inv * kind[1])).astype(o_ref.dtype)


def _gdn_proj(h, w_in, conv_w, streams, n_out, seq, tm, tn, name):
    m, k = h.shape
    sub = _tile(tm, GDN_SUB_ROWS, 8)
    n = len(streams)
    conv_streams = [s for s in streams if s[1][0] != "plain"]
    vmem = n * (2 * k * tn * 2 + 3 * tm * tn * 4 + 8 * sub * tn * 4) + 2 * tm * k * 2
    w_specs = [pl.BlockSpec((k, tn), functools.partial(lambda j, i, c0: (0, j + c0), c0=s[0]))
               for s in streams]
    cw_specs = [pl.BlockSpec((conv_w.shape[0], tn), functools.partial(lambda j, i, c0: (0, j + c0), c0=s[0]))
                for s in conv_streams]
    out_spec = pl.BlockSpec((tm, tn), lambda j, i: (i, j))
    return pl.pallas_call(
        functools.partial(_mm_gdn_proj_kernel, kinds=tuple(s[1] for s in streams),
                          tiles_per_seq=seq // tm, sub=sub),
        grid=(n_out // tn, m // tm),
        in_specs=[pl.BlockSpec((tm, k), lambda j, i: (i, 0))] + w_specs + cw_specs,
        out_specs=[out_spec] * n,
        out_shape=[jax.ShapeDtypeStruct((m, n_out), s[2]) for s in streams],
        scratch_shapes=[pltpu.VMEM((tm + HALO, tn), F32) for _ in conv_streams],
        compiler_params=_cparams(2, vmem),
        name=name,
    )(h, *([w_in] * n), *([conv_w] * len(conv_streams)))


def _chunk_masks(blk, ch):
    shift = ch.bit_length() - 1
    r = lax.broadcasted_iota(jnp.int32, (blk, blk), 0)
    c = lax.broadcasted_iota(jnp.int32, (blk, blk), 1)
    same_chunk = jnp.right_shift(r, shift) == jnp.right_shift(c, shift)
    return (jnp.logical_and(same_chunk, r >= c), jnp.logical_and(same_chunk, r > c), same_chunk)


def _mm_gates_kernel(a_ref, w_ref, alog_ref, dt_ref, o_ref, *, n_heads):
    tm = a_ref.shape[0]
    blk = GDN_BLOCK
    lower_incl, _, same_chunk = _chunk_masks(blk, GDN_CHUNK)
    cum_mat = lower_incl.astype(BF16)
    tot_mat = same_chunk.astype(BF16)
    lane = lax.broadcasted_iota(jnp.int32, (blk, HEAD_DIM), 1)
    w = w_ref[...]
    for r0 in range(0, tm, blk):
        y = _dot(a_ref[r0:r0 + blk, :], w)
        g = -jnp.exp(alog_ref[...]) * _softplus(y + dt_ref[...])
        hi, mid, lo = _split3(g)
        cum = _dot(cum_mat, hi) + _dot(cum_mat, mid) + _dot(cum_mat, lo)
        tot = _dot(tot_mat, hi) + _dot(tot_mat, mid) + _dot(tot_mat, lo)
        o_ref[r0:r0 + blk, :] = jnp.where(lane < n_heads, cum,
                                          jnp.where(lane < 2 * n_heads, _sigmoid(y), tot))


def _gdn_gates(h, w_ab, a_log, dt_bias, tm):
    m, k = h.shape
    n_heads = a_log.shape[0]
    tn = HEAD_DIM
    assert 3 * n_heads <= tn and tm % GDN_BLOCK == 0
    w_a, w_b = w_ab[:, :n_heads], w_ab[:, n_heads:]
    w_pad = jnp.concatenate([w_a, w_b, w_a, jnp.zeros((k, tn - 3 * n_heads), F32)], axis=1).astype(BF16)
    zeros_h = jnp.zeros((n_heads,), F32)
    pad = lambda v: jnp.concatenate([v, zeros_h, v, jnp.zeros((tn - 3 * n_heads,), F32)]).reshape(1, tn)
    vmem = 2 * k * tn * 2 + 2 * tm * k * 2 + 4 * tm * tn * 4
    return pl.pallas_call(
        functools.partial(_mm_gates_kernel, n_heads=n_heads),
        grid=(m // tm,),
        in_specs=[pl.BlockSpec((tm, k), lambda i: (i, 0)),
                  pl.BlockSpec((k, tn), lambda i: (0, 0)),
                  pl.BlockSpec((1, tn), lambda i: (0, 0)),
                  pl.BlockSpec((1, tn), lambda i: (0, 0))],
        out_specs=pl.BlockSpec((tm, tn), lambda i: (i, 0)),
        out_shape=jax.ShapeDtypeStruct((m, tn), F32),
        compiler_params=_cparams(1, vmem),
        name="gdn_gates",
    )(h, w_pad, pad(a_log), pad(dt_bias))


def _sb_kernel(q_ref, k_ref, v_ref, o_ref, *, tq, n_pack, scale):
    qi = pl.program_id(2)
    row = lax.broadcasted_iota(jnp.int32, (tq, tq), 0)
    col = lax.broadcasted_iota(jnp.int32, (tq, tq), 1)
    later = (row > col).astype(BF16)
    below_diag = col < row
    head_cols = [slice(p * HEAD_DIM, (p + 1) * HEAD_DIM) for p in range(n_pack)]
    qs = [q_ref[:, cols] for cols in head_cols]

    def body(carry):
        kb, cs, accs, _ = carry
        start = pl.multiple_of(kb * tq, tq)
        strict = jnp.logical_or(kb < qi, below_diag)
        heads = range(n_pack)
        z = [_dot_nt(qs[p], k_ref[pl.ds(start, tq), head_cols[p]]) * scale for p in heads]
        sp = [jnp.maximum(z[p], 0.0) + jnp.log(1.0 + jnp.exp(-jnp.abs(z[p]))) for p in heads]
        log_1m = [jnp.where(strict, -sp[p], 0.0) for p in heads]
        tail = [_dot(log_1m[p].astype(BF16), later) + cs[p] for p in heads]
        w = [jnp.where(strict, jnp.exp(z[p] - sp[p] + tail[p]), 0.0).astype(BF16) for p in heads]
        new_accs = [accs[p] + _dot(w[p], v_ref[pl.ds(start, tq), head_cols[p]]) for p in heads]
        new_cs = [cs[p] + jnp.sum(log_1m[p], axis=1, keepdims=True) for p in heads]
        c_max = functools.reduce(jnp.maximum, [jnp.max(c) for c in new_cs])
        return kb - 1, tuple(new_cs), tuple(new_accs), c_max > -F32_EXP_UNDERFLOW

    def cond(carry):
        kb, _, _, live = carry
        return jnp.logical_and(kb >= 0, live)

    init = (qi, tuple(jnp.zeros((tq, 1), F32) for _ in range(n_pack)),
            tuple(jnp.zeros((tq, HEAD_DIM), F32) for _ in range(n_pack)), jnp.bool_(True))
    _, _, accs, _ = lax.while_loop(cond, body, init)
    for cols, acc in zip(head_cols, accs):
        o_ref[:, cols] = acc.astype(o_ref.dtype)


def _stick_breaking(proj, bsz, seq, n_heads, tq):
    t = proj.shape[0]
    nq = seq // tq
    n_pack = SB_HEAD_PACK if n_heads % SB_HEAD_PACK == 0 else 1
    groups = n_heads // n_pack
    width = n_pack * HEAD_DIM
    vmem = 2 * 2 * seq * width * 2 + 4 * tq * width * 2 + n_pack * 12 * tq * tq * 4
    return pl.pallas_call(
        functools.partial(_sb_kernel, tq=tq, n_pack=n_pack, scale=HEAD_DIM ** -0.5),
        grid=(bsz, groups, nq),
        in_specs=[pl.BlockSpec((tq, width), lambda b, h, i: (b * nq + i, h)),
                  pl.BlockSpec((seq, width), lambda b, h, i: (b, groups + h)),
                  pl.BlockSpec((seq, width), lambda b, h, i: (b, 2 * groups + h))],
        out_specs=pl.BlockSpec((tq, width), lambda b, h, i: (b * nq + i, h)),
        out_shape=jax.ShapeDtypeStruct((t, n_heads * HEAD_DIM), BF16),
        compiler_params=_cparams(3, vmem),
        name="stick_breaking",
    )(proj, proj, proj)


def _sgu_kernel(u_ref, v_ref, gain_ref, w_ref, b_ref, o_ref):
    rows = u_ref.shape[0]
    r = lax.broadcasted_iota(jnp.int32, (SGU_LEN, SGU_LEN), 0)
    c = lax.broadcasted_iota(jnp.int32, (SGU_LEN, SGU_LEN), 1)
    w = jnp.where(r >= c, w_ref[0], 0.0).astype(BF16)
    bias = b_ref[0]
    gain = gain_ref[0]
    for n in range(rows // SGU_LEN):
        sl = slice(n * SGU_LEN, (n + 1) * SGU_LEN)
        u = _gelu(u_ref[sl, :].astype(F32))
        v = _gelu(v_ref[sl, :].astype(F32))
        v = v * lax.rsqrt(jnp.mean(v * v, axis=-1, keepdims=True) + EPS) * gain
        mixed = _dot(w, v.astype(BF16)) + bias
        o_ref[sl, :] = (u * mixed).astype(o_ref.dtype)


def _spatial_gating(proj, sgu_gain, sgu_w, sgu_b, n_groups, u_block0, rows):
    t = proj.shape[0]
    return pl.pallas_call(
        _sgu_kernel,
        grid=(n_groups, t // rows),
        in_specs=[pl.BlockSpec((rows, HEAD_DIM), lambda g, i: (i, u_block0 + g)),
                  pl.BlockSpec((rows, HEAD_DIM), lambda g, i: (i, u_block0 + n_groups + g)),
                  pl.BlockSpec((1, 1, HEAD_DIM), lambda g, i: (g, 0, 0)),
                  pl.BlockSpec((1, SGU_LEN, SGU_LEN), lambda g, i: (g, 0, 0)),
                  pl.BlockSpec((1, SGU_LEN, 1), lambda g, i: (g, 0, 0))],
        out_specs=pl.BlockSpec((rows, HEAD_DIM), lambda g, i: (i, g)),
        out_shape=jax.ShapeDtypeStruct((t, n_groups * HEAD_DIM), BF16),
        compiler_params=_cparams(2, 16 * rows * HEAD_DIM * 4),
        name="spatial_gating",
    )(proj, proj, sgu_gain.reshape(n_groups, 1, HEAD_DIM), sgu_w,
      sgu_b.reshape(n_groups, SGU_LEN, 1))


def _neumann_solve(p_mats, xs, n_factors):
    idx = range(len(xs))
    n = p_mats[0].shape[0]
    eye = (lax.broadcasted_iota(jnp.int32, (n, n), 0)
           == lax.broadcasted_iota(jnp.int32, (n, n), 1)).astype(F32)
    ts = [eye + p_mats[i] for i in idx]
    for _ in range(1, n_factors):
        pb = [p_mats[i].astype(BF16) for i in idx]
        p_mats = [_dot(pb[i], pb[i]) for i in idx]
        ts = [ts[i] + _dot(p_mats[i].astype(BF16), ts[i].astype(BF16)) for i in idx]
    return [_dot(ts[i].astype(BF16), xs[i].astype(BF16)) for i in idx]


def _gdn_kernel(q_ref, k_ref, v_ref, z_ref, gb_ref, gain_ref, o_ref, state_ref, *, n_heads, n_pack):
    hg = pl.program_id(1)
    n_rows = q_ref.shape[0]
    blk, ch = GDN_BLOCK, GDN_CHUNK
    n_ch = blk // ch
    n_factors = ch.bit_length() - 1

    @pl.when(pl.program_id(2) == 0)
    def _():
        state_ref[...] = jnp.zeros(state_ref.shape, F32)

    lower_incl, lower_strict, _ = _chunk_masks(blk, ch)
    lane = lax.broadcasted_iota(jnp.int32, (blk, HEAD_DIM), 1)
    gain = gain_ref[...]

    def pick(gb, lane_idx):
        return jnp.sum(jnp.where(lane == lane_idx, gb, 0.0), axis=1, keepdims=True)

    def diff_operands(gc):
        chi, cmid, clo = (part.astype(F32) for part in _split3(jnp.broadcast_to(gc, (blk, HEAD_DIM))))
        left = jnp.where(lane == 0, chi, jnp.where(lane == 1, cmid, jnp.where(
            lane == 2, clo, jnp.where(lane < 6, 1.0, 0.0))))
        right = jnp.where(lane == 3, -chi, jnp.where(lane == 4, -cmid, jnp.where(
            lane == 5, -clo, jnp.where(lane < 3, 1.0, 0.0))))
        return left.astype(BF16), right.astype(BF16)

    head_cols = [slice(p * HEAD_DIM, (p + 1) * HEAD_DIM) for p in range(n_pack)]
    heads = range(n_pack)

    def block(b, states):
        rows = pl.ds(pl.multiple_of(b * blk, blk), blk)
        gb = gb_ref[rows, :]
        q = [q_ref[rows, cols].astype(F32) for cols in head_cols]
        k = [k_ref[rows, cols].astype(F32) for cols in head_cols]
        v = [v_ref[rows, cols].astype(F32) for cols in head_cols]
        gc = [pick(gb, hg * n_pack + p) for p in heads]
        beta = [pick(gb, hg * n_pack + p + n_heads) for p in heads]
        gl = [pick(gb, hg * n_pack + p + 2 * n_heads) for p in heads]
        diff = [_dot_nt(*diff_operands(g)) for g in gc]
        decay = [jnp.exp(jnp.where(lower_incl, d, -1e30)) for d in diff]
        eg = [jnp.exp(g) for g in gc]
        kb = [k[p] * beta[p] for p in heads]
        kbf = [k[p].astype(BF16) for p in heads]
        p_mat = [jnp.where(lower_strict, _dot_nt((-kb[p]).astype(BF16), kbf[p]) * decay[p], 0.0)
                 for p in heads]
        qk = [(_dot_nt(q[p].astype(BF16), kbf[p]) * decay[p]).astype(BF16) for p in heads]
        x = [jnp.concatenate([v[p] * beta[p], kb[p] * eg[p]], axis=1) for p in heads]
        x = _neumann_solve(p_mat, x, n_factors)
        u = [x[p][:, :HEAD_DIM] for p in heads]
        w = [x[p][:, HEAD_DIM:].astype(BF16) for p in heads]
        q_dec = [(q[p] * eg[p]).astype(BF16) for p in heads]
        k_dec = [(k[p] * jnp.exp(gl[p] - gc[p])).astype(BF16) for p in heads]
        g_tot = [jnp.exp(gl[p]) for p in heads]

        states = list(states)
        o_inter = [[] for _ in heads]
        v_new = [[] for _ in heads]
        for ci in range(n_ch):
            sl = slice(ci * ch, (ci + 1) * ch)
            sb = [states[p].astype(BF16) for p in heads]
            v_c = [u[p][sl] - _dot(w[p][sl], sb[p]) for p in heads]
            for p in heads:
                o_inter[p].append(_dot(q_dec[p][sl], sb[p]))
                v_new[p].append(v_c[p])
            states = [states[p] * g_tot[p][ci * ch:ci * ch + 1, :]
                      + _dot_tn(k_dec[p][sl], v_c[p].astype(BF16)) for p in heads]
        o = [jnp.concatenate(o_inter[p], axis=0)
             + _dot(qk[p], jnp.concatenate(v_new[p], axis=0).astype(BF16)) for p in heads]
        for p, cols in enumerate(head_cols):
            on = o[p] * lax.rsqrt(jnp.mean(o[p] * o[p], axis=-1, keepdims=True) + EPS) * gain
            o_ref[rows, cols] = (on * _silu(z_ref[rows, cols].astype(F32))).astype(o_ref.dtype)
        return tuple(states)

    states = lax.fori_loop(0, n_rows // blk, block, tuple(state_ref[p] for p in range(n_pack)))
    for p in range(n_pack):
        state_ref[p] = states[p]


def _gated_delta(q, k, v, z, gates, o_gain, bsz, seq, n_heads):
    t = q.shape[0]
    n_pack = GDN_HEAD_PACK if n_heads % GDN_HEAD_PACK == 0 else 1
    width = n_pack * HEAD_DIM
    rows = _tile(seq, 2048, GDN_BLOCK)
    tiles = seq // rows
    head_spec = pl.BlockSpec((rows, width), lambda b, h, r: (b * tiles + r, h))
    io_bytes = sum(jnp.dtype(a.dtype).itemsize for a in (q, k, v, z)) + 2
    vmem = 2 * (rows * width * io_bytes + rows * HEAD_DIM * 4) + n_pack * 40 * GDN_BLOCK * GDN_BLOCK * 4
    return pl.pallas_call(
        functools.partial(_gdn_kernel, n_heads=n_heads, n_pack=n_pack),
        grid=(bsz, n_heads // n_pack, tiles),
        in_specs=[head_spec, head_spec, head_spec, head_spec,
                  pl.BlockSpec((rows, HEAD_DIM), lambda b, h, r: (b * tiles + r, 0)),
                  pl.BlockSpec((1, HEAD_DIM), lambda b, h, r: (0, 0))],
        out_specs=head_spec,
        out_shape=jax.ShapeDtypeStruct((t, n_heads * HEAD_DIM), BF16),
        scratch_shapes=[pltpu.VMEM((n_pack, HEAD_DIM, HEAD_DIM), F32)],
        compiler_params=_cparams(3, vmem),
        name="gated_delta",
    )(q, k, v, z, gates, o_gain.reshape(1, HEAD_DIM))


def kernel(x, c, ada_w, ada_b, ada_layer, norm_mix, norm_ffn, norm_final, ev_w_in, ev_w_out, sgu_gain, sgu_w, sgu_b, gdn_w_in, gdn_conv, gdn_a_log, gdn_dt_bias, gdn_o_gain, gdn_w_out, ffn_w_up, ffn_conv, ffn_conv_b, ffn_w_down):
    bsz, seq, d = x.shape
    depth = ada_layer.shape[0]
    t = bsz * seq
    d_ff = ffn_w_down.shape[1]
    n_sb = d // 2 // HEAD_DIM
    n_sgu = sgu_gain.shape[1]
    n_gdn = gdn_a_log.shape[1]
    assert sgu_gain.shape[2] == HEAD_DIM and d == n_gdn * HEAD_DIM and sgu_w.shape[2] == SGU_LEN
    assert n_sgu == n_sb and seq % GDN_BLOCK == 0
    tm = _tile(seq, 1024)

    def mixer_weights(layer):
        if layer % 2 == 0:
            return (ev_w_in, layer // 2), (ev_w_out, layer // 2)
        return (gdn_w_in, layer // 2), (gdn_w_out, layer // 2)

    mix_in0, mix_out0 = mixer_weights(0)
    w_mix_in, w_mix_out, w_up = _to_bf16(*mix_in0), _to_bf16(*mix_out0), _to_bf16(ffn_w_up, 0)
    gdn_w_ab = gdn_w_in[:, :, 4 * d:]

    mod_rows = _ada_project(c, ada_w, ada_b).reshape(bsz * N_MOD, 1, d)
    layer_rows = ada_layer.reshape(depth * N_MOD, 1, d)
    x = x.reshape(t, d)

    for layer in range(depth):
        h = _norm_modulate(x, norm_mix[layer], mod_rows, layer_rows, layer, 0, seq)
        if layer % 2 == 0:
            e = layer // 2
            n_proj = 5 * (d // 2)
            proj = _matmul(h, w_mix_in, 0, n_proj, BF16, tm, _tile(n_proj, 1024), "even_in_proj")
            o_a = _stick_breaking(proj, bsz, seq, n_sb, _tile(seq, 256))
            o_b = _spatial_gating(proj, sgu_gain[e], sgu_w[e], sgu_b[e], n_sgu, 3 * n_sb,
                                  _tile(seq, 2048))
            x, _ = _matmul_residual([o_a, o_b], w_mix_out, x, mod_rows, layer_rows, layer, 2, seq,
                                    tm, _tile(d, 512), "even_out_proj")
        else:
            o = layer // 2
            tn = _tile(d, 512)
            nb = d // tn
            conv_w = gdn_conv[o]
            q, k = _gdn_proj(h, w_mix_in, conv_w,
                             [(0, ("l2", HEAD_DIM ** -0.5), BF16), (nb, ("l2", 1.0), BF16)],
                             d, seq, tm, tn, "gdn_qk_proj")
            v, z = _gdn_proj(h, w_mix_in, conv_w,
                             [(2 * nb, ("silu",), F32), (3 * nb, ("plain",), F32)],
                             d, seq, tm, tn, "gdn_vz_proj")
            gates = _gdn_gates(h, gdn_w_ab[o], gdn_a_log[o], gdn_dt_bias[o], tm)
            y = _gated_delta(q, k, v, z, gates, gdn_o_gain[o], bsz, seq, n_gdn)
            x, _ = _matmul_residual([y], w_mix_out, x, mod_rows, layer_rows, layer, 2, seq,
                                    tm, _tile(d, 512), "gdn_out_proj")
        has_next = layer + 1 < depth
        next_in, next_out = mixer_weights(layer + 1) if has_next else (None, None)
        h = _norm_modulate(x, norm_ffn[layer], mod_rows, layer_rows, layer, 3, seq)
        f, cast = _ffn_up(h, w_up, ffn_conv[layer], ffn_conv_b[layer], seq, tm, _tile(d_ff, 512),
                          side=[(ffn_w_down, layer)] + ([next_in] if has_next else []))
        w_down = cast[0]
        if has_next:
            w_mix_in = cast[1]
        x, cast = _matmul_residual([f], w_down, x, mod_rows, layer_rows, layer, 5, seq,
                                   _tile(seq, 512), _tile(d, 512), "ffn_down_proj",
                                   side=[next_out, (ffn_w_up, layer + 1)] if has_next else [])
        if has_next:
            w_mix_out, w_up = cast
    return _final_norm(x, norm_final).reshape(bsz, seq, d)
```

```python
import functools
import math

import jax
import jax.numpy as jnp
from jax import lax
from jax.experimental import pallas as pl
from jax.experimental.pallas import tpu as pltpu

F32 = jnp.float32
BF16 = jnp.bfloat16

HEAD_DIM = 128
SGU_LEN = 128
GDN_CHUNK = 64
GDN_BLOCK = 256
SB_HEAD_PACK = 4
GDN_HEAD_PACK = 4
N_MOD = 6
EPS = 1e-6
F32_EXP_UNDERFLOW = 104.0
V7X_VMEM_BYTES = 64 * 1024 * 1024
VMEM_CAP = V7X_VMEM_BYTES - 6 * 1024 * 1024
MIB = 1024 * 1024


def _cparams(n_axes, vmem_bytes):
    limit = int(min(VMEM_CAP, max(32 * MIB, vmem_bytes * 5 // 4 + 4 * MIB)))
    return pltpu.CompilerParams(dimension_semantics=("arbitrary",) * n_axes,
                                vmem_limit_bytes=limit)


def _tile(n, pref, unit=128):
    if n <= pref:
        return n
    t = pref // unit * unit
    while n % t:
        t -= unit
    assert t > 0, (n, pref)
    return t


def _dot(a, b):
    return jnp.dot(a, b, preferred_element_type=F32)


def _dot_nt(a, b):
    return lax.dot_general(a, b, (((1,), (1,)), ((), ())), preferred_element_type=F32)


def _dot_tn(a, b):
    return lax.dot_general(a, b, (((0,), (0,)), ((), ())), preferred_element_type=F32)


def _split3(x):
    hi = x.astype(BF16)
    r = x - hi.astype(F32)
    mid = r.astype(BF16)
    lo = (r - mid.astype(F32)).astype(BF16)
    return hi, mid, lo


def _softplus(x):
    return jnp.maximum(x, 0.0) + jnp.log1p(jnp.exp(-jnp.abs(x)))


def _sigmoid(x):
    return 1.0 / (1.0 + jnp.exp(-x))


def _silu(x):
    return x * _sigmoid(x)


def _gelu(x):
    c = math.sqrt(2.0 / math.pi)
    return x * (0.5 * (1.0 + jnp.tanh(c * (x + 0.044715 * (x * x * x)))))


def _cast_kernel(w_ref, o_ref):
    o_ref[...] = w_ref[...].astype(o_ref.dtype)


def _to_bf16(w, layer):
    _, rows, cols = w.shape
    tr = _tile(rows, max(16, 4 * MIB // (cols * 4)), unit=16)
    return pl.pallas_call(
        _cast_kernel,
        grid=(rows // tr,),
        in_specs=[pl.BlockSpec((None, tr, cols), lambda i: (layer, i, 0))],
        out_specs=pl.BlockSpec((tr, cols), lambda i: (i, 0)),
        out_shape=jax.ShapeDtypeStruct((rows, cols), BF16),
        compiler_params=_cparams(1, 2 * tr * cols * 6),
        name="weights_to_bf16",
    )(w)


def _side_cast_plan(side, n_inner, n_steps):
    in_specs, out_specs, out_shapes, vmem = [], [], [], 0
    for w, layer, transposed in side:
        step = lambda j, i, last: jnp.minimum(j * n_inner + i, last)
        if transposed:
            _, cols, rows = w.shape
            tc = HEAD_DIM * pl.cdiv(pl.cdiv(cols, HEAD_DIM), n_steps)
            last = pl.cdiv(cols, tc) - 1
            in_specs.append(pl.BlockSpec((None, tc, rows), functools.partial(
                lambda j, i, layer, last: (layer, step(j, i, last), 0), layer=layer, last=last)))
            out_specs.append(pl.BlockSpec((rows, tc), functools.partial(
                lambda j, i, last: (0, step(j, i, last)), last=last)))
            vmem += 2 * tc * rows * 6 + tc * rows * 4
        else:
            _, rows, cols = w.shape
            tr = 16
            while rows % tr or rows // tr > n_steps:
                tr += 16
            last = rows // tr - 1
            in_specs.append(pl.BlockSpec((None, tr, cols), functools.partial(
                lambda j, i, layer, last: (layer, step(j, i, last), 0), layer=layer, last=last)))
            out_specs.append(pl.BlockSpec((tr, cols), functools.partial(
                lambda j, i, last: (step(j, i, last), 0), last=last)))
            vmem += 2 * tr * cols * 6
        out_shapes.append(jax.ShapeDtypeStruct((rows, cols), BF16))
    return in_specs, out_specs, out_shapes, vmem


def _side_cast(in_refs, out_refs, transposed):
    for w_ref, o_ref, flip in zip(in_refs, out_refs, transposed):
        w = w_ref[...]
        o_ref[...] = (w.T if flip else w).astype(o_ref.dtype)


def _ada_kernel(c_ref, w_ref, b_ref, o_ref):
    c = c_ref[...]
    o_ref[...] = _dot(_silu(c).astype(BF16), w_ref[...].astype(BF16)) + b_ref[...]


def _ada_project(c, ada_w, ada_b):
    bsz, d = c.shape
    n = ada_w.shape[1]
    rows = 8
    tn = _tile(n, 512)
    c_pad = jnp.zeros((rows, d), F32).at[:bsz].set(c)
    out = pl.pallas_call(
        _ada_kernel,
        grid=(n // tn,),
        in_specs=[pl.BlockSpec((rows, d), lambda j: (0, 0)),
                  pl.BlockSpec((d, tn), lambda j: (0, j)),
                  pl.BlockSpec((1, tn), lambda j: (0, j))],
        out_specs=pl.BlockSpec((rows, tn), lambda j: (0, j)),
        out_shape=jax.ShapeDtypeStruct((rows, n), F32),
        compiler_params=_cparams(1, 2 * d * tn * 4 + d * tn * 2),
        name="ada_project",
    )(c_pad, ada_w, ada_b.reshape(1, n))
    return out[:bsz]


def _norm_mod_kernel(x_ref, g_ref, sh_ref, sc_ref, lsh_ref, lsc_ref, o_ref):
    x = x_ref[...]
    y = x * lax.rsqrt(jnp.mean(x * x, axis=-1, keepdims=True) + EPS) * g_ref[...]
    scale = 1.0 + (sc_ref[0] + lsc_ref[0])
    shift = sh_ref[0] + lsh_ref[0]
    o_ref[...] = (y * scale + shift).astype(o_ref.dtype)


def _norm_modulate(x, gain, mod_rows, layer_rows, layer, shift_idx, seq):
    t, d = x.shape
    tr = _tile(seq, 512)
    per_seq = seq // tr
    mod_spec = lambda idx: pl.BlockSpec((1, 1, d), lambda i: ((i // per_seq) * N_MOD + idx, 0, 0))
    lay_spec = lambda idx: pl.BlockSpec((1, 1, d), lambda i: (layer * N_MOD + idx, 0, 0))
    return pl.pallas_call(
        _norm_mod_kernel,
        grid=(t // tr,),
        in_specs=[pl.BlockSpec((tr, d), lambda i: (i, 0)),
                  pl.BlockSpec((1, d), lambda i: (0, 0)),
                  mod_spec(shift_idx), mod_spec(shift_idx + 1),
                  lay_spec(shift_idx), lay_spec(shift_idx + 1)],
        out_specs=pl.BlockSpec((tr, d), lambda i: (i, 0)),
        out_shape=jax.ShapeDtypeStruct((t, d), BF16),
        compiler_params=_cparams(1, 2 * tr * d * 6),
        name="norm_modulate",
    )(x, gain.reshape(1, d), mod_rows, mod_rows, layer_rows, layer_rows)


def _final_norm_kernel(x_ref, g_ref, o_ref):
    x = x_ref[...]
    o_ref[...] = x * lax.rsqrt(jnp.mean(x * x, axis=-1, keepdims=True) + EPS) * g_ref[...]


def _final_norm(x, gain):
    t, d = x.shape
    tr = _tile(t, 256)
    return pl.pallas_call(
        _final_norm_kernel,
        grid=(t // tr,),
        in_specs=[pl.BlockSpec((tr, d), lambda i: (i, 0)),
                  pl.BlockSpec((1, d), lambda i: (0, 0))],
        out_specs=pl.BlockSpec((tr, d), lambda i: (i, 0)),
        out_shape=jax.ShapeDtypeStruct((t, d), F32),
        compiler_params=_cparams(1, 2 * tr * d * 8),
        name="final_norm",
    )(x, gain.reshape(1, d))


def _mm_plain_kernel(a_ref, w_ref, o_ref):
    o_ref[...] = _dot(a_ref[...], w_ref[...]).astype(o_ref.dtype)


def _matmul(a, w, col_block0, n_out, out_dtype, tm, tn, name):
    m, k = a.shape
    out_bytes = jnp.dtype(out_dtype).itemsize
    vmem = 2 * k * tn * 2 + 2 * tm * k * 2 + 2 * tm * tn * out_bytes + tm * tn * 4
    return pl.pallas_call(
        _mm_plain_kernel,
        grid=(n_out // tn, m // tm),
        in_specs=[pl.BlockSpec((tm, k), lambda j, i: (i, 0)),
                  pl.BlockSpec((k, tn), lambda j, i: (0, j + col_block0))],
        out_specs=pl.BlockSpec((tm, tn), lambda j, i: (i, j)),
        out_shape=jax.ShapeDtypeStruct((m, n_out), out_dtype),
        compiler_params=_cparams(2, vmem),
        name=name,
    )(a, w)


def _mm_resid_kernel(*refs, n_parts, side_t):
    n_side = len(side_t)
    a_refs = refs[:n_parts]
    w_refs = refs[n_parts:2 * n_parts]
    x_ref, gate_ref, lgate_ref = refs[2 * n_parts:2 * n_parts + 3]
    side_in = refs[2 * n_parts + 3:2 * n_parts + 3 + n_side]
    o_ref = refs[2 * n_parts + 3 + n_side]
    side_out = refs[2 * n_parts + 4 + n_side:]
    _side_cast(side_in, side_out, side_t)
    y = _dot(a_refs[0][...], w_refs[0][...])
    for a_ref, w_ref in zip(a_refs[1:], w_refs[1:]):
        y = y + _dot(a_ref[...], w_ref[...])
    o_ref[...] = x_ref[...] + (gate_ref[0] + lgate_ref[0]) * y


def _matmul_residual(a_parts, w, x, mod_rows, layer_rows, layer, gate_idx, seq, tm, tn, name, side=()):
    n_parts = len(a_parts)
    m, kp = a_parts[0].shape
    n = w.shape[1]
    per_seq = seq // tm
    grid = (n // tn, m // tm)
    side_in, side_out, side_shapes, side_vmem = _side_cast_plan(side, grid[1], grid[0] * grid[1])
    vmem = n_parts * (2 * kp * tn * 2 + 2 * tm * kp * 2) + 5 * tm * tn * 4 + side_vmem
    a_specs = [pl.BlockSpec((tm, kp), lambda j, i: (i, 0)) for _ in range(n_parts)]
    w_specs = [pl.BlockSpec((kp, tn), functools.partial(lambda j, i, p: (p, j), p=p))
               for p in range(n_parts)]
    out, *cast = pl.pallas_call(
        functools.partial(_mm_resid_kernel, n_parts=n_parts, side_t=tuple(s[2] for s in side)),
        grid=grid,
        in_specs=a_specs + w_specs + [
            pl.BlockSpec((tm, tn), lambda j, i: (i, j)),
            pl.BlockSpec((1, 1, tn), lambda j, i: ((i // per_seq) * N_MOD + gate_idx, 0, j)),
            pl.BlockSpec((1, 1, tn), lambda j, i: (layer * N_MOD + gate_idx, 0, j))] + side_in,
        out_specs=[pl.BlockSpec((tm, tn), lambda j, i: (i, j))] + side_out,
        out_shape=[jax.ShapeDtypeStruct((m, n), F32)] + side_shapes,
        compiler_params=_cparams(2, vmem),
        name=name,
    )(*a_parts, *([w] * n_parts), x, mod_rows, layer_rows, *[s[0] for s in side])
    return out, cast


HALO = 8
FFN_SUB_ROWS = 256
GDN_SUB_ROWS = 128


def _conv_begin(ybuf_ref, tm, tiles_per_seq):
    starts_sequence = (pl.program_id(1) % tiles_per_seq) == 0

    @pl.when(starts_sequence)
    def _():
        ybuf_ref[0:HALO, :] = jnp.zeros((HALO, ybuf_ref.shape[1]), F32)

    @pl.when(jnp.logical_not(starts_sequence))
    def _():
        ybuf_ref[0:HALO, :] = ybuf_ref[tm:tm + HALO, :]


def _conv_rows(y, ybuf_ref, cw_ref, r0):
    rows = y.shape[0]
    taps = cw_ref.shape[0]
    ybuf_ref[HALO + r0:HALO + r0 + rows, :] = y
    out = cw_ref[taps - 1:taps, :] * y
    for j in range(taps - 1):
        off = HALO + r0 - (taps - 1) + j
        out = out + cw_ref[j:j + 1, :] * ybuf_ref[off:off + rows, :]
    return out


def _mm_ffn_up_kernel(*refs, tiles_per_seq, sub, side_t):
    n_side = len(side_t)
    a_ref, wg_ref, wv_ref, cw_ref, cb_ref = refs[:5]
    side_in = refs[5:5 + n_side]
    o_ref = refs[5 + n_side]
    side_out = refs[6 + n_side:6 + 2 * n_side]
    ybuf_ref = refs[6 + 2 * n_side]
    _side_cast(side_in, side_out, side_t)
    tm = a_ref.shape[0]
    _conv_begin(ybuf_ref, tm, tiles_per_seq)
    for r0 in range(0, tm, sub):
        a = a_ref[r0:r0 + sub, :]
        gate = _conv_rows(_dot(a, wg_ref[...]), ybuf_ref, cw_ref, r0) + cb_ref[...]
        o_ref[r0:r0 + sub, :] = (_gelu(gate) * _dot(a, wv_ref[...])).astype(o_ref.dtype)


def _ffn_up(h, w_up, conv_w, conv_b, seq, tm, tn, side=()):
    m, k = h.shape
    d_ff = w_up.shape[1] // 2
    nb = d_ff // tn
    sub = _tile(tm, FFN_SUB_ROWS, 8)
    grid = (nb, m // tm)
    side_in, side_out, side_shapes, side_vmem = _side_cast_plan(side, grid[1], grid[0] * grid[1])
    vmem = 2 * 2 * k * tn * 2 + 2 * tm * k * 2 + 3 * tm * tn * 4 + 8 * sub * tn * 4 + side_vmem
    out, *cast = pl.pallas_call(
        functools.partial(_mm_ffn_up_kernel, tiles_per_seq=seq // tm, sub=sub,
                          side_t=tuple(s[2] for s in side)),
        grid=grid,
        in_specs=[pl.BlockSpec((tm, k), lambda j, i: (i, 0)),
                  pl.BlockSpec((k, tn), lambda j, i: (0, j)),
                  pl.BlockSpec((k, tn), lambda j, i: (0, j + nb)),
                  pl.BlockSpec((conv_w.shape[0], tn), lambda j, i: (0, j)),
                  pl.BlockSpec((1, tn), lambda j, i: (0, j))] + side_in,
        out_specs=[pl.BlockSpec((tm, tn), lambda j, i: (i, j))] + side_out,
        out_shape=[jax.ShapeDtypeStruct((m, d_ff), BF16)] + side_shapes,
        scratch_shapes=[pltpu.VMEM((tm + HALO, tn), F32)],
        compiler_params=_cparams(2, vmem),
        name="ffn_up",
    )(h, w_up, w_up, conv_w, conv_b.reshape(1, d_ff), *[s[0] for s in side])
    return out, cast


def _mm_gdn_proj_kernel(*refs, kinds, tiles_per_seq, sub):
    n = len(kinds)
    n_conv = sum(kind[0] != "plain" for kind in kinds)
    a_ref = refs[0]
    w_refs = refs[1:1 + n]
    cw_refs = iter(refs[1 + n:1 + n + n_conv])
    o_refs = refs[1 + n + n_conv:1 + 2 * n + n_conv]
    ybuf_refs = iter(refs[1 + 2 * n + n_conv:])
    conv_refs = [None if kind[0] == "plain" else (next(cw_refs), next(ybuf_refs)) for kind in kinds]
    tm = a_ref.shape[0]
    for conv in conv_refs:
        if conv is not None:
            _conv_begin(conv[1], tm, tiles_per_seq)
    for r0 in range(0, tm, sub):
        a = a_ref[r0:r0 + sub, :]
        for kind, w_ref, conv, o_ref in zip(kinds, w_refs, conv_refs, o_refs):
            y = _dot(a, w_ref[...])
            if conv is not None:
                y = _silu(_conv_rows(y, conv[1], conv[0], r0))
            if kind[0] != "l2":
                o_ref[r0:r0 + sub, :] = y.astype(o_ref.dtype)
                continue
            for g in range(y.shape[1] // HEAD_DIM):
                cols = slice(g * HEAD_DIM, (g + 1) * HEAD_DIM)
                yg = y[:, cols]
                inv = lax.rsqrt(jnp.sum(yg * yg, axis=-1, keepdims=True) + EPS)
                o_ref[r0:r0 + sub, cols] = (yg * (inv * kind[1])).astype(o_ref.dtype)


def _gdn_proj(h, w_in, conv_w, streams, n_out, seq, tm, tn, name):
    m, k = h.shape
    sub = _tile(tm, GDN_SUB_ROWS, 8)
    n = len(streams)
    conv_streams = [s for s in streams if s[1][0] != "plain"]
    vmem = n * (2 * k * tn * 2 + 3 * tm * tn * 4 + 8 * sub * tn * 4) + 2 * tm * k * 2
    w_specs = [pl.BlockSpec((k, tn), functools.partial(lambda j, i, c0: (0, j + c0), c0=s[0]))
               for s in streams]
    cw_specs = [pl.BlockSpec((conv_w.shape[0], tn), functools.partial(lambda j, i, c0: (0, j + c0), c0=s[0]))
                for s in conv_streams]
    out_spec = pl.BlockSpec((tm, tn), lambda j, i: (i, j))
    return pl.pallas_call(
        functools.partial(_mm_gdn_proj_kernel, kinds=tuple(s[1] for s in streams),
                          tiles_per_seq=seq // tm, sub=sub),
        grid=(n_out // tn, m // tm),
        in_specs=[pl.BlockSpec((tm, k), lambda j, i: (i, 0))] + w_specs + cw_specs,
        out_specs=[out_spec] * n,
        out_shape=[jax.ShapeDtypeStruct((m, n_out), s[2]) for s in streams],
        scratch_shapes=[pltpu.VMEM((tm + HALO, tn), F32) for _ in conv_streams],
        compiler_params=_cparams(2, vmem),
        name=name,
    )(h, *([w_in] * n), *([conv_w] * len(conv_streams)))


def _chunk_masks(blk, ch):
    shift = ch.bit_length() - 1
    r = lax.broadcasted_iota(jnp.int32, (blk, blk), 0)
    c = lax.broadcasted_iota(jnp.int32, (blk, blk), 1)
    same_chunk = jnp.right_shift(r, shift) == jnp.right_shift(c, shift)
    return (jnp.logical_and(same_chunk, r >= c), jnp.logical_and(same_chunk, r > c), same_chunk)


def _mm_gates_kernel(a_ref, w_ref, alog_ref, dt_ref, o_ref, *, n_heads):
    tm = a_ref.shape[0]
    blk = GDN_BLOCK
    lower_incl, _, same_chunk = _chunk_masks(blk, GDN_CHUNK)
    cum_mat = lower_incl.astype(BF16)
    tot_mat = same_chunk.astype(BF16)
    lane = lax.broadcasted_iota(jnp.int32, (blk, HEAD_DIM), 1)
    w_ab = w_ref[...]
    w = jnp.concatenate([w_ab, w_ab[:n_heads],
                         jnp.zeros((HEAD_DIM - 3 * n_heads, w_ab.shape[1]), F32)], axis=0).astype(BF16)
    for r0 in range(0, tm, blk):
        y = _dot_nt(a_ref[r0:r0 + blk, :], w)
        g = -jnp.exp(alog_ref[...]) * _softplus(y + dt_ref[...])
        hi, mid, lo = _split3(g)
        cum = _dot(cum_mat, hi) + _dot(cum_mat, mid) + _dot(cum_mat, lo)
        tot = _dot(tot_mat, hi) + _dot(tot_mat, mid) + _dot(tot_mat, lo)
        o_ref[r0:r0 + blk, :] = jnp.where(lane < n_heads, cum,
                                          jnp.where(lane < 2 * n_heads, _sigmoid(y), tot))


def _gdn_gates(h, w_in_t, layer, a_log, dt_bias, tm):
    m, k = h.shape
    n_heads = a_log.shape[0]
    tn = HEAD_DIM
    n_rows = w_in_t.shape[1]
    assert 3 * n_heads <= tn and tm % GDN_BLOCK == 0 and n_rows % (2 * n_heads) == 0
    zeros_h = jnp.zeros((n_heads,), F32)
    pad = lambda v: jnp.concatenate([v, zeros_h, v, jnp.zeros((tn - 3 * n_heads,), F32)]).reshape(1, tn)
    vmem = 4 * tn * k * 4 + 2 * tm * k * 2 + 4 * tm * tn * 4
    return pl.pallas_call(
        functools.partial(_mm_gates_kernel, n_heads=n_heads),
        grid=(m // tm,),
        in_specs=[pl.BlockSpec((tm, k), lambda i: (i, 0)),
                  pl.BlockSpec((None, 2 * n_heads, k), lambda i: (layer, n_rows // (2 * n_heads) - 1, 0)),
                  pl.BlockSpec((1, tn), lambda i: (0, 0)),
                  pl.BlockSpec((1, tn), lambda i: (0, 0))],
        out_specs=pl.BlockSpec((tm, tn), lambda i: (i, 0)),
        out_shape=jax.ShapeDtypeStruct((m, tn), F32),
        compiler_params=_cparams(1, vmem),
        name="gdn_gates",
    )(h, w_in_t, pad(a_log), pad(dt_bias))


def _sb_kernel(q_ref, k_ref, v_ref, o_ref, *, tq, n_pack, scale):
    qi = pl.program_id(2)
    row = lax.broadcasted_iota(jnp.int32, (tq, tq), 0)
    col = lax.broadcasted_iota(jnp.int32, (tq, tq), 1)
    later = (row > col).astype(BF16)
    below_diag = col < row
    head_cols = [slice(p * HEAD_DIM, (p + 1) * HEAD_DIM) for p in range(n_pack)]
    qs = [q_ref[:, cols] for cols in head_cols]

    def body(carry):
        kb, cs, accs, _ = carry
        start = pl.multiple_of(kb * tq, tq)
        strict = jnp.logical_or(kb < qi, below_diag)
        heads = range(n_pack)
        z = [_dot_nt(qs[p], k_ref[pl.ds(start, tq), head_cols[p]]) * scale for p in heads]
        sp = [jnp.maximum(z[p], 0.0) + jnp.log(1.0 + jnp.exp(-jnp.abs(z[p]))) for p in heads]
        log_beta = [z[p] - sp[p] for p in heads]
        log_1m = [jnp.where(strict, -sp[p], 0.0) for p in heads]
        tail = [_dot(log_1m[p].astype(BF16), later) + cs[p] for p in heads]
        w = [jnp.where(strict, jnp.exp(log_beta[p] + tail[p]), 0.0).astype(BF16) for p in heads]
        new_accs = [accs[p] + _dot(w[p], v_ref[pl.ds(start, tq), head_cols[p]]) for p in heads]
        new_cs = [cs[p] + jnp.sum(log_1m[p], axis=1, keepdims=True) for p in heads]
        c_max = functools.reduce(jnp.maximum, [jnp.max(c) for c in new_cs])
        return kb - 1, tuple(new_cs), tuple(new_accs), c_max > -F32_EXP_UNDERFLOW

    def cond(carry):
        kb, _, _, live = carry
        return jnp.logical_and(kb >= 0, live)

    init = (qi, tuple(jnp.zeros((tq, 1), F32) for _ in range(n_pack)),
            tuple(jnp.zeros((tq, HEAD_DIM), F32) for _ in range(n_pack)), jnp.bool_(True))
    _, _, accs, _ = lax.while_loop(cond, body, init)
    for cols, acc in zip(head_cols, accs):
        o_ref[:, cols] = acc.astype(o_ref.dtype)


def _stick_breaking(proj, bsz, seq, n_heads, tq):
    t = proj.shape[0]
    nq = seq // tq
    n_pack = SB_HEAD_PACK if n_heads % SB_HEAD_PACK == 0 else 1
    groups = n_heads // n_pack
    width = n_pack * HEAD_DIM
    vmem = 2 * 2 * seq * width * 2 + 4 * tq * width * 2 + n_pack * 12 * tq * tq * 4
    return pl.pallas_call(
        functools.partial(_sb_kernel, tq=tq, n_pack=n_pack, scale=HEAD_DIM ** -0.5),
        grid=(bsz, groups, nq),
        in_specs=[pl.BlockSpec((tq, width), lambda b, h, i: (b * nq + i, h)),
                  pl.BlockSpec((seq, width), lambda b, h, i: (b, groups + h)),
                  pl.BlockSpec((seq, width), lambda b, h, i: (b, 2 * groups + h))],
        out_specs=pl.BlockSpec((tq, width), lambda b, h, i: (b * nq + i, h)),
        out_shape=jax.ShapeDtypeStruct((t, n_heads * HEAD_DIM), BF16),
        compiler_params=_cparams(3, vmem),
        name="stick_breaking",
    )(proj, proj, proj)


def _sgu_kernel(u_ref, v_ref, gain_ref, w_ref, b_ref, o_ref):
    rows = u_ref.shape[0]
    r = lax.broadcasted_iota(jnp.int32, (SGU_LEN, SGU_LEN), 0)
    c = lax.broadcasted_iota(jnp.int32, (SGU_LEN, SGU_LEN), 1)
    w = jnp.where(r >= c, w_ref[0], 0.0).astype(BF16)
    bias = b_ref[0]
    gain = gain_ref[0]
    for n in range(rows // SGU_LEN):
        sl = slice(n * SGU_LEN, (n + 1) * SGU_LEN)
        u = _gelu(u_ref[sl, :].astype(F32))
        v = _gelu(v_ref[sl, :].astype(F32))
        v = v * lax.rsqrt(jnp.mean(v * v, axis=-1, keepdims=True) + EPS) * gain
        mixed = _dot(w, v.astype(BF16)) + bias
        o_ref[sl, :] = (u * mixed).astype(o_ref.dtype)


def _spatial_gating(proj, sgu_gain, sgu_w, sgu_b, n_groups, u_block0, rows):
    t = proj.shape[0]
    return pl.pallas_call(
        _sgu_kernel,
        grid=(n_groups, t // rows),
        in_specs=[pl.BlockSpec((rows, HEAD_DIM), lambda g, i: (i, u_block0 + g)),
                  pl.BlockSpec((rows, HEAD_DIM), lambda g, i: (i, u_block0 + n_groups + g)),
                  pl.BlockSpec((1, 1, HEAD_DIM), lambda g, i: (g, 0, 0)),
                  pl.BlockSpec((1, SGU_LEN, SGU_LEN), lambda g, i: (g, 0, 0)),
                  pl.BlockSpec((1, SGU_LEN, 1), lambda g, i: (g, 0, 0))],
        out_specs=pl.BlockSpec((rows, HEAD_DIM), lambda g, i: (i, g)),
        out_shape=jax.ShapeDtypeStruct((t, n_groups * HEAD_DIM), BF16),
        compiler_params=_cparams(2, 16 * rows * HEAD_DIM * 4),
        name="spatial_gating",
    )(proj, proj, sgu_gain.reshape(n_groups, 1, HEAD_DIM), sgu_w,
      sgu_b.reshape(n_groups, SGU_LEN, 1))


def _neumann_solve(p_mats, xs, n_factors):
    idx = range(len(xs))
    n = p_mats[0].shape[0]
    eye = (lax.broadcasted_iota(jnp.int32, (n, n), 0)
           == lax.broadcasted_iota(jnp.int32, (n, n), 1)).astype(F32)
    ts = [eye + p_mats[i] for i in idx]
    for _ in range(1, n_factors):
        pb = [p_mats[i].astype(BF16) for i in idx]
        p_mats = [_dot(pb[i], pb[i]) for i in idx]
        ts = [ts[i] + _dot(p_mats[i].astype(BF16), ts[i].astype(BF16)) for i in idx]
    return [_dot(ts[i].astype(BF16), xs[i].astype(BF16)) for i in idx]


def _gdn_kernel(q_ref, k_ref, v_ref, z_ref, gb_ref, gain_ref, o_ref, state_ref, *, n_heads, n_pack):
    hg = pl.program_id(1)
    n_rows = q_ref.shape[0]
    blk, ch = GDN_BLOCK, GDN_CHUNK
    n_ch = blk // ch
    n_factors = ch.bit_length() - 1

    @pl.when(pl.program_id(2) == 0)
    def _():
        state_ref[...] = jnp.zeros(state_ref.shape, F32)

    lower_incl, lower_strict, _ = _chunk_masks(blk, ch)
    lane = lax.broadcasted_iota(jnp.int32, (blk, HEAD_DIM), 1)
    gain = gain_ref[...]

    def pick(gb, lane_idx):
        return jnp.sum(jnp.where(lane == lane_idx, gb, 0.0), axis=1, keepdims=True)

    def diff_operands(gc):
        chi, cmid, clo = (part.astype(F32) for part in _split3(jnp.broadcast_to(gc, (blk, HEAD_DIM))))
        left = jnp.where(lane == 0, chi, jnp.where(lane == 1, cmid, jnp.where(
            lane == 2, clo, jnp.where(lane < 6, 1.0, 0.0))))
        right = jnp.where(lane == 3, -chi, jnp.where(lane == 4, -cmid, jnp.where(
            lane == 5, -clo, jnp.where(lane < 3, 1.0, 0.0))))
        return left.astype(BF16), right.astype(BF16)

    head_cols = [slice(p * HEAD_DIM, (p + 1) * HEAD_DIM) for p in range(n_pack)]
    heads = range(n_pack)

    def block(b, states):
        rows = pl.ds(pl.multiple_of(b * blk, blk), blk)
        gb = gb_ref[rows, :]
        q = [q_ref[rows, cols].astype(F32) for cols in head_cols]
        k = [k_ref[rows, cols].astype(F32) for cols in head_cols]
        v = [v_ref[rows, cols].astype(F32) for cols in head_cols]
        gc = [pick(gb, hg * n_pack + p) for p in heads]
        beta = [pick(gb, hg * n_pack + p + n_heads) for p in heads]
        gl = [pick(gb, hg * n_pack + p + 2 * n_heads) for p in heads]
        diff = [_dot_nt(*diff_operands(g)) for g in gc]
        decay = [jnp.exp(jnp.where(lower_incl, d, -1e30)) for d in diff]
        eg = [jnp.exp(g) for g in gc]
        kb = [k[p] * beta[p] for p in heads]
        kbf = [k[p].astype(BF16) for p in heads]
        p_mat = [jnp.where(lower_strict, _dot_nt((-kb[p]).astype(BF16), kbf[p]) * decay[p], 0.0)
                 for p in heads]
        qk = [(_dot_nt(q[p].astype(BF16), kbf[p]) * decay[p]).astype(BF16) for p in heads]
        x = [jnp.concatenate([v[p] * beta[p], kb[p] * eg[p]], axis=1) for p in heads]
        x = _neumann_solve(p_mat, x, n_factors)
        u = [x[p][:, :HEAD_DIM] for p in heads]
        w = [x[p][:, HEAD_DIM:].astype(BF16) for p in heads]
        q_dec = [(q[p] * eg[p]).astype(BF16) for p in heads]
        k_dec = [(k[p] * jnp.exp(gl[p] - gc[p])).astype(BF16) for p in heads]
        g_tot = [jnp.exp(gl[p]) for p in heads]

        states = list(states)
        o_inter = [[] for _ in heads]
        v_new = [[] for _ in heads]
        for ci in range(n_ch):
            sl = slice(ci * ch, (ci + 1) * ch)
            sb = [states[p].astype(BF16) for p in heads]
            v_c = [u[p][sl] - _dot(w[p][sl], sb[p]) for p in heads]
            for p in heads:
                o_inter[p].append(_dot(q_dec[p][sl], sb[p]))
                v_new[p].append(v_c[p])
            states = [states[p] * g_tot[p][ci * ch:ci * ch + 1, :]
                      + _dot_tn(k_dec[p][sl], v_c[p].astype(BF16)) for p in heads]
        o = [jnp.concatenate(o_inter[p], axis=0)
             + _dot(qk[p], jnp.concatenate(v_new[p], axis=0).astype(BF16)) for p in heads]
        for p, cols in enumerate(head_cols):
            on = o[p] * lax.rsqrt(jnp.mean(o[p] * o[p], axis=-1, keepdims=True) + EPS) * gain
            o_ref[rows, cols] = (on * _silu(z_ref[rows, cols].astype(F32))).astype(o_ref.dtype)
        return tuple(states)

    states = lax.fori_loop(0, n_rows // blk, block, tuple(state_ref[p] for p in range(n_pack)))
    for p in range(n_pack):
        state_ref[p] = states[p]


def _gated_delta(q, k, v, z, gates, o_gain, bsz, seq, n_heads):
    t = q.shape[0]
    n_pack = GDN_HEAD_PACK if n_heads % GDN_HEAD_PACK == 0 else 1
    width = n_pack * HEAD_DIM
    rows = _tile(seq, 2048, GDN_BLOCK)
    tiles = seq // rows
    head_spec = pl.BlockSpec((rows, width), lambda b, h, r: (b * tiles + r, h))
    io_bytes = sum(jnp.dtype(a.dtype).itemsize for a in (q, k, v, z)) + 2
    vmem = 2 * (rows * width * io_bytes + rows * HEAD_DIM * 4) + n_pack * 40 * GDN_BLOCK * GDN_BLOCK * 4
    return pl.pallas_call(
        functools.partial(_gdn_kernel, n_heads=n_heads, n_pack=n_pack),
        grid=(bsz, n_heads // n_pack, tiles),
        in_specs=[head_spec, head_spec, head_spec, head_spec,
                  pl.BlockSpec((rows, HEAD_DIM), lambda b, h, r: (b * tiles + r, 0)),
                  pl.BlockSpec((1, HEAD_DIM), lambda b, h, r: (0, 0))],
        out_specs=head_spec,
        out_shape=jax.ShapeDtypeStruct((t, n_heads * HEAD_DIM), BF16),
        scratch_shapes=[pltpu.VMEM((n_pack, HEAD_DIM, HEAD_DIM), F32)],
        compiler_params=_cparams(3, vmem),
        name="gated_delta",
    )(q, k, v, z, gates, o_gain.reshape(1, HEAD_DIM))


def kernel(x, c, ada_w, ada_b, ada_layer, norm_mix, norm_ffn, norm_final, ev_w_in, ev_w_out, sgu_gain, sgu_w, sgu_b, gdn_w_in, gdn_conv, gdn_a_log, gdn_dt_bias, gdn_o_gain, gdn_w_out, ffn_w_up, ffn_conv, ffn_conv_b, ffn_w_down):
    bsz, seq, d = x.shape
    depth = ada_layer.shape[0]
    t = bsz * seq
    d_ff = ffn_w_down.shape[1]
    n_sb = d // 2 // HEAD_DIM
    n_sgu = sgu_gain.shape[1]
    n_gdn = gdn_a_log.shape[1]
    assert sgu_gain.shape[2] == HEAD_DIM and d == n_gdn * HEAD_DIM and sgu_w.shape[2] == SGU_LEN
    assert n_sgu == n_sb and seq % GDN_BLOCK == 0
    tm = _tile(seq, 1024)

    gdn_w_in_t = jnp.swapaxes(gdn_w_in, 1, 2)

    def mixer_weights(layer):
        if layer % 2 == 0:
            return (ev_w_in, layer // 2, False), (ev_w_out, layer // 2, False)
        return (gdn_w_in_t, layer // 2, True), (gdn_w_out, layer // 2, False)

    assert depth > 0
    w_mix_in, w_mix_out, w_up = _to_bf16(ev_w_in, 0), _to_bf16(ev_w_out, 0), _to_bf16(ffn_w_up, 0)

    mod_rows = _ada_project(c, ada_w, ada_b).reshape(bsz * N_MOD, 1, d)
    layer_rows = ada_layer.reshape(depth * N_MOD, 1, d)
    x = x.reshape(t, d)

    for layer in range(depth):
        h = _norm_modulate(x, norm_mix[layer], mod_rows, layer_rows, layer, 0, seq)
        if layer % 2 == 0:
            e = layer // 2
            n_proj = 5 * (d // 2)
            proj = _matmul(h, w_mix_in, 0, n_proj, BF16, tm, _tile(n_proj, 1024), "even_in_proj")
            o_a = _stick_breaking(proj, bsz, seq, n_sb, _tile(seq, 256))
            o_b = _spatial_gating(proj, sgu_gain[e], sgu_w[e], sgu_b[e], n_sgu, 3 * n_sb,
                                  _tile(seq, 2048))
            x, _ = _matmul_residual([o_a, o_b], w_mix_out, x, mod_rows, layer_rows, layer, 2, seq,
                                    tm, _tile(d, 512), "even_out_proj")
        else:
            o = layer // 2
            tn = _tile(d, 512)
            nb = d // tn
            conv_w = gdn_conv[o]
            q, k = _gdn_proj(h, w_mix_in, conv_w,
                             [(0, ("l2", HEAD_DIM ** -0.5), BF16), (nb, ("l2", 1.0), BF16)],
                             d, seq, tm, tn, "gdn_qk_proj")
            v, z = _gdn_proj(h, w_mix_in, conv_w,
                             [(2 * nb, ("silu",), F32), (3 * nb, ("plain",), F32)],
                             d, seq, tm, tn, "gdn_vz_proj")
            gates = _gdn_gates(h, gdn_w_in_t, o, gdn_a_log[o], gdn_dt_bias[o], tm)
            y = _gated_delta(q, k, v, z, gates, gdn_o_gain[o], bsz, seq, n_gdn)
            x, _ = _matmul_residual([y], w_mix_out, x, mod_rows, layer_rows, layer, 2, seq,
                                    tm, _tile(d, 512), "gdn_out_proj")
        has_next = layer + 1 < depth
        side = [(ffn_w_down, layer, False)]
        if has_next:
            side += [*mixer_weights(layer + 1), (ffn_w_up, layer + 1, False)]
        h = _norm_modulate(x, norm_ffn[layer], mod_rows, layer_rows, layer, 3, seq)
        f, cast = _ffn_up(h, w_up, ffn_conv[layer], ffn_conv_b[layer], seq, tm, _tile(d_ff, 512),
                          side=side)
        w_down = cast[0]
        if has_next:
            w_mix_in, w_mix_out, w_up = cast[1:]
        x, _ = _matmul_residual([f], w_down, x, mod_rows, layer_rows, layer, 5, seq,
                                _tile(seq, 512), _tile(d, 512), "ffn_down_proj")
    return _final_norm(x, norm_final).reshape(bsz, seq, d)
```

```python
import functools
import math

import jax
import jax.numpy as jnp
from jax import lax
from jax.experimental import pallas as pl
from jax.experimental.pallas import tpu as pltpu

F32 = jnp.float32
BF16 = jnp.bfloat16

HEAD_DIM = 128
SGU_LEN = 128
GDN_CHUNK = 64
GDN_BLOCK = 256
SB_HEAD_PACK = 4
GDN_HEAD_PACK = 4
N_MOD = 6
EPS = 1e-6
F32_EXP_UNDERFLOW = 104.0
V7X_VMEM_BYTES = 64 * 1024 * 1024
VMEM_CAP = V7X_VMEM_BYTES - 6 * 1024 * 1024
MIB = 1024 * 1024


def _cparams(n_axes, vmem_bytes):
    limit = int(min(VMEM_CAP, max(32 * MIB, vmem_bytes * 5 // 4 + 4 * MIB)))
    return pltpu.CompilerParams(dimension_semantics=("arbitrary",) * n_axes,
                                vmem_limit_bytes=limit)


def _tile(n, pref, unit=128):
    if n <= pref:
        return n
    t = pref // unit * unit
    while n % t:
        t -= unit
    assert t > 0, (n, pref)
    return t


def _dot(a, b):
    return jnp.dot(a, b, preferred_element_type=F32)


def _dot_nt(a, b):
    return lax.dot_general(a, b, (((1,), (1,)), ((), ())), preferred_element_type=F32)


def _dot_tn(a, b):
    return lax.dot_general(a, b, (((0,), (0,)), ((), ())), preferred_element_type=F32)


def _split3(x):
    hi = x.astype(BF16)
    r = x - hi.astype(F32)
    mid = r.astype(BF16)
    lo = (r - mid.astype(F32)).astype(BF16)
    return hi, mid, lo


def _softplus(x):
    return jnp.maximum(x, 0.0) + jnp.log1p(jnp.exp(-jnp.abs(x)))


def _sigmoid(x):
    return 1.0 / (1.0 + jnp.exp(-x))


def _silu(x):
    return x * _sigmoid(x)


def _gelu(x):
    c = math.sqrt(2.0 / math.pi)
    return x * (0.5 * (1.0 + jnp.tanh(c * (x + 0.044715 * (x * x * x)))))


def _cast_kernel(w_ref, o_ref):
    o_ref[...] = w_ref[...].astype(o_ref.dtype)


def _to_bf16(w, layer):
    _, rows, cols = w.shape
    tr = _tile(rows, max(16, 4 * MIB // (cols * 4)), unit=16)
    return pl.pallas_call(
        _cast_kernel,
        grid=(rows // tr,),
        in_specs=[pl.BlockSpec((None, tr, cols), lambda i: (layer, i, 0))],
        out_specs=pl.BlockSpec((tr, cols), lambda i: (i, 0)),
        out_shape=jax.ShapeDtypeStruct((rows, cols), BF16),
        compiler_params=_cparams(1, 2 * tr * cols * 6),
        name="weights_to_bf16",
    )(w)


def _side_cast_plan(side, n_inner, n_steps):
    in_specs, out_specs, out_shapes, vmem = [], [], [], 0
    for w, layer, transposed in side:
        step = lambda j, i, last: jnp.minimum(j * n_inner + i, last)
        if transposed:
            _, cols, rows = w.shape
            tc = HEAD_DIM * pl.cdiv(pl.cdiv(cols, HEAD_DIM), n_steps)
            last = pl.cdiv(cols, tc) - 1
            in_specs.append(pl.BlockSpec((None, tc, rows), functools.partial(
                lambda j, i, layer, last: (layer, step(j, i, last), 0), layer=layer, last=last)))
            out_specs.append(pl.BlockSpec((rows, tc), functools.partial(
                lambda j, i, last: (0, step(j, i, last)), last=last)))
            vmem += 2 * tc * rows * 6 + tc * rows * 4
        else:
            _, rows, cols = w.shape
            tr = 16
            while rows % tr or rows // tr > n_steps:
                tr += 16
            last = rows // tr - 1
            in_specs.append(pl.BlockSpec((None, tr, cols), functools.partial(
                lambda j, i, layer, last: (layer, step(j, i, last), 0), layer=layer, last=last)))
            out_specs.append(pl.BlockSpec((tr, cols), functools.partial(
                lambda j, i, last: (step(j, i, last), 0), last=last)))
            vmem += 2 * tr * cols * 6
        out_shapes.append(jax.ShapeDtypeStruct((rows, cols), BF16))
    return in_specs, out_specs, out_shapes, vmem


def _side_cast(in_refs, out_refs, transposed):
    for w_ref, o_ref, flip in zip(in_refs, out_refs, transposed):
        w = w_ref[...]
        o_ref[...] = (w.T if flip else w).astype(o_ref.dtype)


def _ada_kernel(c_ref, w_ref, b_ref, o_ref):
    c = c_ref[...]
    o_ref[...] = _dot(_silu(c).astype(BF16), w_ref[...].astype(BF16)) + b_ref[...]


def _ada_project(c, ada_w, ada_b):
    bsz, d = c.shape
    n = ada_w.shape[1]
    rows = 8
    tn = _tile(n, 512)
    c_pad = jnp.zeros((rows, d), F32).at[:bsz].set(c)
    out = pl.pallas_call(
        _ada_kernel,
        grid=(n // tn,),
        in_specs=[pl.BlockSpec((rows, d), lambda j: (0, 0)),
                  pl.BlockSpec((d, tn), lambda j: (0, j)),
                  pl.BlockSpec((1, tn), lambda j: (0, j))],
        out_specs=pl.BlockSpec((rows, tn), lambda j: (0, j)),
        out_shape=jax.ShapeDtypeStruct((rows, n), F32),
        compiler_params=_cparams(1, 2 * d * tn * 4 + d * tn * 2),
        name="ada_project",
    )(c_pad, ada_w, ada_b.reshape(1, n))
    return out[:bsz]


def _norm_mod_kernel(x_ref, g_ref, sh_ref, sc_ref, lsh_ref, lsc_ref, o_ref):
    x = x_ref[...]
    y = x * lax.rsqrt(jnp.mean(x * x, axis=-1, keepdims=True) + EPS) * g_ref[...]
    scale = 1.0 + (sc_ref[0] + lsc_ref[0])
    shift = sh_ref[0] + lsh_ref[0]
    o_ref[...] = (y * scale + shift).astype(o_ref.dtype)


def _norm_modulate(x, gain, mod_rows, layer_rows, layer, shift_idx, seq):
    t, d = x.shape
    tr = _tile(seq, 512)
    per_seq = seq // tr
    mod_spec = lambda idx: pl.BlockSpec((1, 1, d), lambda i: ((i // per_seq) * N_MOD + idx, 0, 0))
    lay_spec = lambda idx: pl.BlockSpec((1, 1, d), lambda i: (layer * N_MOD + idx, 0, 0))
    return pl.pallas_call(
        _norm_mod_kernel,
        grid=(t // tr,),
        in_specs=[pl.BlockSpec((tr, d), lambda i: (i, 0)),
                  pl.BlockSpec((1, d), lambda i: (0, 0)),
                  mod_spec(shift_idx), mod_spec(shift_idx + 1),
                  lay_spec(shift_idx), lay_spec(shift_idx + 1)],
        out_specs=pl.BlockSpec((tr, d), lambda i: (i, 0)),
        out_shape=jax.ShapeDtypeStruct((t, d), BF16),
        compiler_params=_cparams(1, 2 * tr * d * 6),
        name="norm_modulate",
    )(x, gain.reshape(1, d), mod_rows, mod_rows, layer_rows, layer_rows)


def _final_norm_kernel(x_ref, g_ref, o_ref):
    x = x_ref[...]
    o_ref[...] = x * lax.rsqrt(jnp.mean(x * x, axis=-1, keepdims=True) + EPS) * g_ref[...]


def _final_norm(x, gain):
    t, d = x.shape
    tr = _tile(t, 256)
    return pl.pallas_call(
        _final_norm_kernel,
        grid=(t // tr,),
        in_specs=[pl.BlockSpec((tr, d), lambda i: (i, 0)),
                  pl.BlockSpec((1, d), lambda i: (0, 0))],
        out_specs=pl.BlockSpec((tr, d), lambda i: (i, 0)),
        out_shape=jax.ShapeDtypeStruct((t, d), F32),
        compiler_params=_cparams(1, 2 * tr * d * 8),
        name="final_norm",
    )(x, gain.reshape(1, d))


def _mm_plain_kernel(a_ref, w_ref, o_ref):
    o_ref[...] = _dot(a_ref[...], w_ref[...]).astype(o_ref.dtype)


def _matmul(a, w, col_block0, n_out, out_dtype, tm, tn, name):
    m, k = a.shape
    out_bytes = jnp.dtype(out_dtype).itemsize
    vmem = 2 * k * tn * 2 + 2 * tm * k * 2 + 2 * tm * tn * out_bytes + tm * tn * 4
    return pl.pallas_call(
        _mm_plain_kernel,
        grid=(n_out // tn, m // tm),
        in_specs=[pl.BlockSpec((tm, k), lambda j, i: (i, 0)),
                  pl.BlockSpec((k, tn), lambda j, i: (0, j + col_block0))],
        out_specs=pl.BlockSpec((tm, tn), lambda j, i: (i, j)),
        out_shape=jax.ShapeDtypeStruct((m, n_out), out_dtype),
        compiler_params=_cparams(2, vmem),
        name=name,
    )(a, w)


def _mm_resid_kernel(*refs, n_parts, side_t):
    n_side = len(side_t)
    a_refs = refs[:n_parts]
    w_refs = refs[n_parts:2 * n_parts]
    x_ref, gate_ref, lgate_ref = refs[2 * n_parts:2 * n_parts + 3]
    side_in = refs[2 * n_parts + 3:2 * n_parts + 3 + n_side]
    o_ref = refs[2 * n_parts + 3 + n_side]
    side_out = refs[2 * n_parts + 4 + n_side:]
    _side_cast(side_in, side_out, side_t)
    y = _dot(a_refs[0][...], w_refs[0][...])
    for a_ref, w_ref in zip(a_refs[1:], w_refs[1:]):
        y = y + _dot(a_ref[...], w_ref[...])
    o_ref[...] = x_ref[...] + (gate_ref[0] + lgate_ref[0]) * y


def _matmul_residual(a_parts, w, x, mod_rows, layer_rows, layer, gate_idx, seq, tm, tn, name, side=()):
    n_parts = len(a_parts)
    m, kp = a_parts[0].shape
    n = w.shape[1]
    per_seq = seq // tm
    grid = (n // tn, m // tm)
    side_in, side_out, side_shapes, side_vmem = _side_cast_plan(side, grid[1], grid[0] * grid[1])
    vmem = n_parts * (2 * kp * tn * 2 + 2 * tm * kp * 2) + 5 * tm * tn * 4 + side_vmem
    a_specs = [pl.BlockSpec((tm, kp), lambda j, i: (i, 0)) for _ in range(n_parts)]
    w_specs = [pl.BlockSpec((kp, tn), functools.partial(lambda j, i, p: (p, j), p=p))
               for p in range(n_parts)]
    out, *cast = pl.pallas_call(
        functools.partial(_mm_resid_kernel, n_parts=n_parts, side_t=tuple(s[2] for s in side)),
        grid=grid,
        in_specs=a_specs + w_specs + [
            pl.BlockSpec((tm, tn), lambda j, i: (i, j)),
            pl.BlockSpec((1, 1, tn), lambda j, i: ((i // per_seq) * N_MOD + gate_idx, 0, j)),
            pl.BlockSpec((1, 1, tn), lambda j, i: (layer * N_MOD + gate_idx, 0, j))] + side_in,
        out_specs=[pl.BlockSpec((tm, tn), lambda j, i: (i, j))] + side_out,
        out_shape=[jax.ShapeDtypeStruct((m, n), F32)] + side_shapes,
        compiler_params=_cparams(2, vmem),
        name=name,
    )(*a_parts, *([w] * n_parts), x, mod_rows, layer_rows, *[s[0] for s in side])
    return out, cast


HALO = 8
FFN_SUB_ROWS = 256
GDN_SUB_ROWS = 128


def _conv_begin(ybuf_ref, tm, tiles_per_seq):
    starts_sequence = (pl.program_id(1) % tiles_per_seq) == 0

    @pl.when(starts_sequence)
    def _():
        ybuf_ref[0:HALO, :] = jnp.zeros((HALO, ybuf_ref.shape[1]), F32)

    @pl.when(jnp.logical_not(starts_sequence))
    def _():
        ybuf_ref[0:HALO, :] = ybuf_ref[tm:tm + HALO, :]


def _conv_rows(y, ybuf_ref, cw_ref, r0):
    rows = y.shape[0]
    taps = cw_ref.shape[0]
    ybuf_ref[HALO + r0:HALO + r0 + rows, :] = y
    out = cw_ref[taps - 1:taps, :] * y
    for j in range(taps - 1):
        off = HALO + r0 - (taps - 1) + j
        out = out + cw_ref[j:j + 1, :] * ybuf_ref[off:off + rows, :]
    return out


def _mm_ffn_up_kernel(*refs, tiles_per_seq, sub, side_t):
    n_side = len(side_t)
    a_ref, wg_ref, wv_ref, cw_ref, cb_ref = refs[:5]
    side_in = refs[5:5 + n_side]
    o_ref = refs[5 + n_side]
    side_out = refs[6 + n_side:6 + 2 * n_side]
    ybuf_ref = refs[6 + 2 * n_side]
    _side_cast(side_in, side_out, side_t)
    tm = a_ref.shape[0]
    _conv_begin(ybuf_ref, tm, tiles_per_seq)
    for r0 in range(0, tm, sub):
        a = a_ref[r0:r0 + sub, :]
        gate = _conv_rows(_dot(a, wg_ref[...]), ybuf_ref, cw_ref, r0) + cb_ref[...]
        o_ref[r0:r0 + sub, :] = (_gelu(gate) * _dot(a, wv_ref[...])).astype(o_ref.dtype)


def _ffn_up(h, w_up, conv_w, conv_b, seq, tm, tn, side=()):
    m, k = h.shape
    d_ff = w_up.shape[1] // 2
    nb = d_ff // tn
    sub = _tile(tm, FFN_SUB_ROWS, 8)
    grid = (nb, m // tm)
    side_in, side_out, side_shapes, side_vmem = _side_cast_plan(side, grid[1], grid[0] * grid[1])
    vmem = 2 * 2 * k * tn * 2 + 2 * tm * k * 2 + 3 * tm * tn * 4 + 8 * sub * tn * 4 + side_vmem
    out, *cast = pl.pallas_call(
        functools.partial(_mm_ffn_up_kernel, tiles_per_seq=seq // tm, sub=sub,
                          side_t=tuple(s[2] for s in side)),
        grid=grid,
        in_specs=[pl.BlockSpec((tm, k), lambda j, i: (i, 0)),
                  pl.BlockSpec((k, tn), lambda j, i: (0, j)),
                  pl.BlockSpec((k, tn), lambda j, i: (0, j + nb)),
                  pl.BlockSpec((conv_w.shape[0], tn), lambda j, i: (0, j)),
                  pl.BlockSpec((1, tn), lambda j, i: (0, j))] + side_in,
        out_specs=[pl.BlockSpec((tm, tn), lambda j, i: (i, j))] + side_out,
        out_shape=[jax.ShapeDtypeStruct((m, d_ff), BF16)] + side_shapes,
        scratch_shapes=[pltpu.VMEM((tm + HALO, tn), F32)],
        compiler_params=_cparams(2, vmem),
        name="ffn_up",
    )(h, w_up, w_up, conv_w, conv_b.reshape(1, d_ff), *[s[0] for s in side])
    return out, cast


def _mm_gdn_proj_kernel(*refs, kinds, tiles_per_seq, sub):
    n = len(kinds)
    n_conv = sum(kind[0] != "plain" for kind in kinds)
    a_ref = refs[0]
    w_refs = refs[1:1 + n]
    cw_refs = iter(refs[1 + n:1 + n + n_conv])
    o_refs = refs[1 + n + n_conv:1 + 2 * n + n_conv]
    ybuf_refs = iter(refs[1 + 2 * n + n_conv:])
    conv_refs = [None if kind[0] == "plain" else (next(cw_refs), next(ybuf_refs)) for kind in kinds]
    tm = a_ref.shape[0]
    for conv in conv_refs:
        if conv is not None:
            _conv_begin(conv[1], tm, tiles_per_seq)
    for r0 in range(0, tm, sub):
        a = a_ref[r0:r0 + sub, :]
        for kind, w_ref, conv, o_ref in zip(kinds, w_refs, conv_refs, o_refs):
            y = _dot(a, w_ref[...])
            if conv is not None:
                y = _silu(_conv_rows(y, conv[1], conv[0], r0))
            if kind[0] != "l2":
                o_ref[r0:r0 + sub, :] = y.astype(o_ref.dtype)
                continue
            for g in range(y.shape[1] // HEAD_DIM):
                cols = slice(g * HEAD_DIM, (g + 1) * HEAD_DIM)
                yg = y[:, cols]
                inv = lax.rsqrt(jnp.sum(yg * yg, axis=-1, keepdims=True) + EPS)
                o_ref[r0:r0 + sub, cols] = (yg * (inv * kind[1])).astype(o_ref.dtype)


def _gdn_proj(h, w_in, conv_w, streams, n_out, seq, tm, tn, name):
    m, k = h.shape
    sub = _tile(tm, GDN_SUB_ROWS, 8)
    n = len(streams)
    conv_streams = [s for s in streams if s[1][0] != "plain"]
    vmem = n * (2 * k * tn * 2 + 3 * tm * tn * 4 + 8 * sub * tn * 4) + 2 * tm * k * 2
    w_specs = [pl.BlockSpec((k, tn), functools.partial(lambda j, i, c0: (0, j + c0), c0=s[0]))
               for s in streams]
    cw_specs = [pl.BlockSpec((conv_w.shape[0], tn), functools.partial(lambda j, i, c0: (0, j + c0), c0=s[0]))
                for s in conv_streams]
    out_spec = pl.BlockSpec((tm, tn), lambda j, i: (i, j))
    return pl.pallas_call(
        functools.partial(_mm_gdn_proj_kernel, kinds=tuple(s[1] for s in streams),
                          tiles_per_seq=seq // tm, sub=sub),
        grid=(n_out // tn, m // tm),
        in_specs=[pl.BlockSpec((tm, k), lambda j, i: (i, 0))] + w_specs + cw_specs,
        out_specs=[out_spec] * n,
        out_shape=[jax.ShapeDtypeStruct((m, n_out), s[2]) for s in streams],
        scratch_shapes=[pltpu.VMEM((tm + HALO, tn), F32) for _ in conv_streams],
        compiler_params=_cparams(2, vmem),
        name=name,
    )(h, *([w_in] * n), *([conv_w] * len(conv_streams)))


def _chunk_masks(blk, ch):
    shift = ch.bit_length() - 1
    r = lax.broadcasted_iota(jnp.int32, (blk, blk), 0)
    c = lax.broadcasted_iota(jnp.int32, (blk, blk), 1)
    same_chunk = jnp.right_shift(r, shift) == jnp.right_shift(c, shift)
    return (jnp.logical_and(same_chunk, r >= c), jnp.logical_and(same_chunk, r > c), same_chunk)


def _mm_gates_kernel(a_ref, w_ref, alog_ref, dt_ref, o_ref, *, n_heads):
    tm = a_ref.shape[0]
    blk = GDN_BLOCK
    lower_incl, _, same_chunk = _chunk_masks(blk, GDN_CHUNK)
    cum_mat = lower_incl.astype(BF16)
    tot_mat = same_chunk.astype(BF16)
    lane = lax.broadcasted_iota(jnp.int32, (blk, HEAD_DIM), 1)
    w_ab = w_ref[...]
    w = jnp.concatenate([w_ab, w_ab[:n_heads],
                         jnp.zeros((HEAD_DIM - 3 * n_heads, w_ab.shape[1]), F32)], axis=0).astype(BF16)
    for r0 in range(0, tm, blk):
        y = _dot_nt(a_ref[r0:r0 + blk, :], w)
        g = -jnp.exp(alog_ref[...]) * _softplus(y + dt_ref[...])
        hi, mid, lo = _split3(g)
        cum = _dot(cum_mat, hi) + _dot(cum_mat, mid) + _dot(cum_mat, lo)
        tot = _dot(tot_mat, hi) + _dot(tot_mat, mid) + _dot(tot_mat, lo)
        o_ref[r0:r0 + blk, :] = jnp.where(lane < n_heads, cum,
                                          jnp.where(lane < 2 * n_heads, _sigmoid(y), tot))


def _gdn_gates(h, w_in_t, layer, a_log, dt_bias, tm):
    m, k = h.shape
    n_heads = a_log.shape[0]
    tn = HEAD_DIM
    n_rows = w_in_t.shape[1]
    assert 3 * n_heads <= tn and tm % GDN_BLOCK == 0 and n_rows % (2 * n_heads) == 0
    zeros_h = jnp.zeros((n_heads,), F32)
    pad = lambda v: jnp.concatenate([v, zeros_h, v, jnp.zeros((tn - 3 * n_heads,), F32)]).reshape(1, tn)
    vmem = 4 * tn * k * 4 + 2 * tm * k * 2 + 4 * tm * tn * 4
    return pl.pallas_call(
        functools.partial(_mm_gates_kernel, n_heads=n_heads),
        grid=(m // tm,),
        in_specs=[pl.BlockSpec((tm, k), lambda i: (i, 0)),
                  pl.BlockSpec((None, 2 * n_heads, k), lambda i: (layer, n_rows // (2 * n_heads) - 1, 0)),
                  pl.BlockSpec((1, tn), lambda i: (0, 0)),
                  pl.BlockSpec((1, tn), lambda i: (0, 0))],
        out_specs=pl.BlockSpec((tm, tn), lambda i: (i, 0)),
        out_shape=jax.ShapeDtypeStruct((m, tn), F32),
        compiler_params=_cparams(1, vmem),
        name="gdn_gates",
    )(h, w_in_t, pad(a_log), pad(dt_bias))


def _sb_kernel(q_ref, k_ref, v_ref, o_ref, *, tq, n_pack, scale):
    qi = pl.program_id(2)
    row = lax.broadcasted_iota(jnp.int32, (tq, tq), 0)
    col = lax.broadcasted_iota(jnp.int32, (tq, tq), 1)
    later = (row > col).astype(BF16)
    below_diag = col < row
    head_cols = [slice(p * HEAD_DIM, (p + 1) * HEAD_DIM) for p in range(n_pack)]
    qs = [q_ref[:, cols] for cols in head_cols]

    def key_tile(kb, cs, accs, on_diagonal):
        start = pl.multiple_of(kb * tq, tq)
        keep = (lambda x: jnp.where(below_diag, x, 0.0)) if on_diagonal else (lambda x: x)
        heads = range(n_pack)
        z = [_dot_nt(qs[p], k_ref[pl.ds(start, tq), head_cols[p]]) * scale for p in heads]
        sp = [jnp.maximum(z[p], 0.0) + jnp.log(1.0 + jnp.exp(-jnp.abs(z[p]))) for p in heads]
        log_beta = [z[p] - sp[p] for p in heads]
        log_1m = [keep(-sp[p]) for p in heads]
        tail = [_dot(log_1m[p].astype(BF16), later) + cs[p] for p in heads]
        w = [keep(jnp.exp(log_beta[p] + tail[p])).astype(BF16) for p in heads]
        new_accs = [accs[p] + _dot(w[p], v_ref[pl.ds(start, tq), head_cols[p]]) for p in heads]
        new_cs = [cs[p] + jnp.sum(log_1m[p], axis=1, keepdims=True) for p in heads]
        c_max = functools.reduce(jnp.maximum, [jnp.max(c) for c in new_cs])
        return kb - 1, tuple(new_cs), tuple(new_accs), c_max > -F32_EXP_UNDERFLOW

    def cond(carry):
        kb, _, _, live = carry
        return jnp.logical_and(kb >= 0, live)

    zeros = lambda width: tuple(jnp.zeros((tq, width), F32) for _ in range(n_pack))
    carry = key_tile(qi, zeros(1), zeros(HEAD_DIM), True)
    _, _, accs, _ = lax.while_loop(cond, lambda c: key_tile(c[0], c[1], c[2], False), carry)
    for cols, acc in zip(head_cols, accs):
        o_ref[:, cols] = acc.astype(o_ref.dtype)


def _stick_breaking(proj, bsz, seq, n_heads, tq):
    t = proj.shape[0]
    nq = seq // tq
    n_pack = SB_HEAD_PACK if n_heads % SB_HEAD_PACK == 0 else 1
    groups = n_heads // n_pack
    width = n_pack * HEAD_DIM
    vmem = 2 * 2 * seq * width * 2 + 4 * tq * width * 2 + n_pack * 12 * tq * tq * 4
    return pl.pallas_call(
        functools.partial(_sb_kernel, tq=tq, n_pack=n_pack, scale=HEAD_DIM ** -0.5),
        grid=(bsz, groups, nq),
        in_specs=[pl.BlockSpec((tq, width), lambda b, h, i: (b * nq + i, h)),
                  pl.BlockSpec((seq, width), lambda b, h, i: (b, groups + h)),
                  pl.BlockSpec((seq, width), lambda b, h, i: (b, 2 * groups + h))],
        out_specs=pl.BlockSpec((tq, width), lambda b, h, i: (b * nq + i, h)),
        out_shape=jax.ShapeDtypeStruct((t, n_heads * HEAD_DIM), BF16),
        compiler_params=_cparams(3, vmem),
        name="stick_breaking",
    )(proj, proj, proj)


def _sgu_kernel(u_ref, v_ref, gain_ref, w_ref, b_ref, o_ref):
    rows = u_ref.shape[0]
    r = lax.broadcasted_iota(jnp.int32, (SGU_LEN, SGU_LEN), 0)
    c = lax.broadcasted_iota(jnp.int32, (SGU_LEN, SGU_LEN), 1)
    w = jnp.where(r >= c, w_ref[0], 0.0).astype(BF16)
    bias = b_ref[0]
    gain = gain_ref[0]
    for n in range(rows // SGU_LEN):
        sl = slice(n * SGU_LEN, (n + 1) * SGU_LEN)
        u = _gelu(u_ref[sl, :].astype(F32))
        v = _gelu(v_ref[sl, :].astype(F32))
        v = v * lax.rsqrt(jnp.mean(v * v, axis=-1, keepdims=True) + EPS) * gain
        mixed = _dot(w, v.astype(BF16)) + bias
        o_ref[sl, :] = (u * mixed).astype(o_ref.dtype)


def _spatial_gating(proj, sgu_gain, sgu_w, sgu_b, n_groups, u_block0, rows):
    t = proj.shape[0]
    return pl.pallas_call(
        _sgu_kernel,
        grid=(n_groups, t // rows),
        in_specs=[pl.BlockSpec((rows, HEAD_DIM), lambda g, i: (i, u_block0 + g)),
                  pl.BlockSpec((rows, HEAD_DIM), lambda g, i: (i, u_block0 + n_groups + g)),
                  pl.BlockSpec((1, 1, HEAD_DIM), lambda g, i: (g, 0, 0)),
                  pl.BlockSpec((1, SGU_LEN, SGU_LEN), lambda g, i: (g, 0, 0)),
                  pl.BlockSpec((1, SGU_LEN, 1), lambda g, i: (g, 0, 0))],
        out_specs=pl.BlockSpec((rows, HEAD_DIM), lambda g, i: (i, g)),
        out_shape=jax.ShapeDtypeStruct((t, n_groups * HEAD_DIM), BF16),
        compiler_params=_cparams(2, 16 * rows * HEAD_DIM * 4),
        name="spatial_gating",
    )(proj, proj, sgu_gain.reshape(n_groups, 1, HEAD_DIM), sgu_w,
      sgu_b.reshape(n_groups, SGU_LEN, 1))


def _neumann_solve(p_mats, xs, n_factors):
    idx = range(len(xs))
    n = p_mats[0].shape[0]
    eye = (lax.broadcasted_iota(jnp.int32, (n, n), 0)
           == lax.broadcasted_iota(jnp.int32, (n, n), 1)).astype(F32)
    ts = [eye + p_mats[i] for i in idx]
    for _ in range(1, n_factors):
        pb = [p_mats[i].astype(BF16) for i in idx]
        p_mats = [_dot(pb[i], pb[i]) for i in idx]
        ts = [ts[i] + _dot(p_mats[i].astype(BF16), ts[i].astype(BF16)) for i in idx]
    return [_dot(ts[i].astype(BF16), xs[i].astype(BF16)) for i in idx]


def _gdn_kernel(q_ref, k_ref, v_ref, z_ref, gb_ref, gain_ref, o_ref, state_ref, *, n_heads, n_pack):
    hg = pl.program_id(1)
    n_rows = q_ref.shape[0]
    blk, ch = GDN_BLOCK, GDN_CHUNK
    n_ch = blk // ch
    n_factors = ch.bit_length() - 1

    @pl.when(pl.program_id(2) == 0)
    def _():
        state_ref[...] = jnp.zeros(state_ref.shape, F32)

    lower_incl, lower_strict, _ = _chunk_masks(blk, ch)
    lane = lax.broadcasted_iota(jnp.int32, (blk, HEAD_DIM), 1)
    gain = gain_ref[...]

    def pick(gb, lane_idx):
        return jnp.sum(jnp.where(lane == lane_idx, gb, 0.0), axis=1, keepdims=True)

    def diff_operands(gc):
        chi, cmid, clo = (part.astype(F32) for part in _split3(jnp.broadcast_to(gc, (blk, HEAD_DIM))))
        left = jnp.where(lane == 0, chi, jnp.where(lane == 1, cmid, jnp.where(
            lane == 2, clo, jnp.where(lane < 6, 1.0, 0.0))))
        right = jnp.where(lane == 3, -chi, jnp.where(lane == 4, -cmid, jnp.where(
            lane == 5, -clo, jnp.where(lane < 3, 1.0, 0.0))))
        return left.astype(BF16), right.astype(BF16)

    head_cols = [slice(p * HEAD_DIM, (p + 1) * HEAD_DIM) for p in range(n_pack)]
    heads = range(n_pack)

    def block(b, states):
        rows = pl.ds(pl.multiple_of(b * blk, blk), blk)
        gb = gb_ref[rows, :]
        q = [q_ref[rows, cols].astype(F32) for cols in head_cols]
        k = [k_ref[rows, cols].astype(F32) for cols in head_cols]
        v = [v_ref[rows, cols].astype(F32) for cols in head_cols]
        gc = [pick(gb, hg * n_pack + p) for p in heads]
        beta = [pick(gb, hg * n_pack + p + n_heads) for p in heads]
        gl = [pick(gb, hg * n_pack + p + 2 * n_heads) for p in heads]
        diff = [_dot_nt(*diff_operands(g)) for g in gc]
        decay = [jnp.exp(jnp.where(lower_incl, d, -1e30)) for d in diff]
        eg = [jnp.exp(g) for g in gc]
        kb = [k[p] * beta[p] for p in heads]
        kbf = [k[p].astype(BF16) for p in heads]
        p_mat = [jnp.where(lower_strict, _dot_nt((-kb[p]).astype(BF16), kbf[p]) * decay[p], 0.0)
                 for p in heads]
        qk = [(_dot_nt(q[p].astype(BF16), kbf[p]) * decay[p]).astype(BF16) for p in heads]
        x = [jnp.concatenate([v[p] * beta[p], kb[p] * eg[p]], axis=1) for p in heads]
        x = _neumann_solve(p_mat, x, n_factors)
        u = [x[p][:, :HEAD_DIM] for p in heads]
        w = [x[p][:, HEAD_DIM:].astype(BF16) for p in heads]
        q_dec = [(q[p] * eg[p]).astype(BF16) for p in heads]
        k_dec = [(k[p] * jnp.exp(gl[p] - gc[p])).astype(BF16) for p in heads]
        g_tot = [jnp.exp(gl[p]) for p in heads]

        states = list(states)
        o_inter = [[] for _ in heads]
        v_new = [[] for _ in heads]
        for ci in range(n_ch):
            sl = slice(ci * ch, (ci + 1) * ch)
            sb = [states[p].astype(BF16) for p in heads]
            v_c = [u[p][sl] - _dot(w[p][sl], sb[p]) for p in heads]
            for p in heads:
                o_inter[p].append(_dot(q_dec[p][sl], sb[p]))
                v_new[p].append(v_c[p])
            states = [states[p] * g_tot[p][ci * ch:ci * ch + 1, :]
                      + _dot_tn(k_dec[p][sl], v_c[p].astype(BF16)) for p in heads]
        o = [jnp.concatenate(o_inter[p], axis=0)
             + _dot(qk[p], jnp.concatenate(v_new[p], axis=0).astype(BF16)) for p in heads]
        for p, cols in enumerate(head_cols):
            on = o[p] * lax.rsqrt(jnp.mean(o[p] * o[p], axis=-1, keepdims=True) + EPS) * gain
            o_ref[rows, cols] = (on * _silu(z_ref[rows, cols].astype(F32))).astype(o_ref.dtype)
        return tuple(states)

    states = lax.fori_loop(0, n_rows // blk, block, tuple(state_ref[p] for p in range(n_pack)),
                           unroll=4)
    for p in range(n_pack):
        state_ref[p] = states[p]


def _gated_delta(q, k, v, z, gates, o_gain, bsz, seq, n_heads):
    t = q.shape[0]
    n_pack = GDN_HEAD_PACK if n_heads % GDN_HEAD_PACK == 0 else 1
    width = n_pack * HEAD_DIM
    rows = _tile(seq, 2048, GDN_BLOCK)
    tiles = seq // rows
    head_spec = pl.BlockSpec((rows, width), lambda b, h, r: (b * tiles + r, h))
    io_bytes = sum(jnp.dtype(a.dtype).itemsize for a in (q, k, v, z)) + 2
    vmem = 2 * (rows * width * io_bytes + rows * HEAD_DIM * 4) + n_pack * 40 * GDN_BLOCK * GDN_BLOCK * 4
    return pl.pallas_call(
        functools.partial(_gdn_kernel, n_heads=n_heads, n_pack=n_pack),
        grid=(bsz, n_heads // n_pack, tiles),
        in_specs=[head_spec, head_spec, head_spec, head_spec,
                  pl.BlockSpec((rows, HEAD_DIM), lambda b, h, r: (b * tiles + r, 0)),
                  pl.BlockSpec((1, HEAD_DIM), lambda b, h, r: (0, 0))],
        out_specs=head_spec,
        out_shape=jax.ShapeDtypeStruct((t, n_heads * HEAD_DIM), BF16),
        scratch_shapes=[pltpu.VMEM((n_pack, HEAD_DIM, HEAD_DIM), F32)],
        compiler_params=_cparams(3, vmem),
        name="gated_delta",
    )(q, k, v, z, gates, o_gain.reshape(1, HEAD_DIM))


def kernel(x, c, ada_w, ada_b, ada_layer, norm_mix, norm_ffn, norm_final, ev_w_in, ev_w_out, sgu_gain, sgu_w, sgu_b, gdn_w_in, gdn_conv, gdn_a_log, gdn_dt_bias, gdn_o_gain, gdn_w_out, ffn_w_up, ffn_conv, ffn_conv_b, ffn_w_down):
    bsz, seq, d = x.shape
    depth = ada_layer.shape[0]
    t = bsz * seq
    d_ff = ffn_w_down.shape[1]
    n_sb = d // 2 // HEAD_DIM
    n_sgu = sgu_gain.shape[1]
    n_gdn = gdn_a_log.shape[1]
    assert sgu_gain.shape[2] == HEAD_DIM and d == n_gdn * HEAD_DIM and sgu_w.shape[2] == SGU_LEN
    assert n_sgu == n_sb and seq % GDN_BLOCK == 0
    tm = _tile(seq, 1024)

    gdn_w_in_t = jnp.swapaxes(gdn_w_in, 1, 2)

    def mixer_weights(layer):
        if layer % 2 == 0:
            return (ev_w_in, layer // 2, False), (ev_w_out, layer // 2, False)
        return (gdn_w_in_t, layer // 2, True), (gdn_w_out, layer // 2, False)

    assert depth > 0
    w_mix_in, w_mix_out, w_up = _to_bf16(ev_w_in, 0), _to_bf16(ev_w_out, 0), _to_bf16(ffn_w_up, 0)

    mod_rows = _ada_project(c, ada_w, ada_b).reshape(bsz * N_MOD, 1, d)
    layer_rows = ada_layer.reshape(depth * N_MOD, 1, d)
    x = x.reshape(t, d)

    for layer in range(depth):
        h = _norm_modulate(x, norm_mix[layer], mod_rows, layer_rows, layer, 0, seq)
        if layer % 2 == 0:
            e = layer // 2
            n_proj = 5 * (d // 2)
            proj = _matmul(h, w_mix_in, 0, n_proj, BF16, tm, _tile(n_proj, 1024), "even_in_proj")
            o_a = _stick_breaking(proj, bsz, seq, n_sb, _tile(seq, 256))
            o_b = _spatial_gating(proj, sgu_gain[e], sgu_w[e], sgu_b[e], n_sgu, 3 * n_sb,
                                  _tile(seq, 2048))
            x, _ = _matmul_residual([o_a, o_b], w_mix_out, x, mod_rows, layer_rows, layer, 2, seq,
                                    tm, _tile(d, 512), "even_out_proj")
        else:
            o = layer // 2
            tn = _tile(d, 512)
            nb = d // tn
            conv_w = gdn_conv[o]
            q, k = _gdn_proj(h, w_mix_in, conv_w,
                             [(0, ("l2", HEAD_DIM ** -0.5), BF16), (nb, ("l2", 1.0), BF16)],
                             d, seq, tm, tn, "gdn_qk_proj")
            v, z = _gdn_proj(h, w_mix_in, conv_w,
                             [(2 * nb, ("silu",), F32), (3 * nb, ("plain",), F32)],
                             d, seq, tm, tn, "gdn_vz_proj")
            gates = _gdn_gates(h, gdn_w_in_t, o, gdn_a_log[o], gdn_dt_bias[o], tm)
            y = _gated_delta(q, k, v, z, gates, gdn_o_gain[o], bsz, seq, n_gdn)
            x, _ = _matmul_residual([y], w_mix_out, x, mod_rows, layer_rows, layer, 2, seq,
                                    tm, _tile(d, 512), "gdn_out_proj")
        has_next = layer + 1 < depth
        side = [(ffn_w_down, layer, False)]
        if has_next:
            side += [*mixer_weights(layer + 1), (ffn_w_up, layer + 1, False)]
        h = _norm_modulate(x, norm_ffn[layer], mod_rows, layer_rows, layer, 3, seq)
        f, cast = _ffn_up(h, w_up, ffn_conv[layer], ffn_conv_b[layer], seq, tm, _tile(d_ff, 512),
                          side=side)
        w_down = cast[0]
        if has_next:
            w_mix_in, w_mix_out, w_up = cast[1:]
        x, _ = _matmul_residual([f], w_down, x, mod_rows, layer_rows, layer, 5, seq,
                                _tile(seq, 512), _tile(d, 512), "ffn_down_proj")
    return _final_norm(x, norm_final).reshape(bsz, seq, d)
```

```python
import functools
import math

import jax
import jax.numpy as jnp
from jax import lax
from jax.experimental import pallas as pl
from jax.experimental.pallas import tpu as pltpu

F32 = jnp.float32
BF16 = jnp.bfloat16

HEAD_DIM = 128
SGU_LEN = 128
GDN_CHUNK = 64
GDN_BLOCK = 256
SB_HEAD_PACK = 4
GDN_HEAD_PACK = 4
N_MOD = 6
EPS = 1e-6
F32_EXP_UNDERFLOW = 104.0
V7X_VMEM_BYTES = 64 * 1024 * 1024
VMEM_CAP = V7X_VMEM_BYTES - 6 * 1024 * 1024
MIB = 1024 * 1024


def _cparams(n_axes, vmem_bytes):
    limit = int(min(VMEM_CAP, max(32 * MIB, vmem_bytes * 5 // 4 + 4 * MIB)))
    return pltpu.CompilerParams(dimension_semantics=("arbitrary",) * n_axes,
                                vmem_limit_bytes=limit)


def _tile(n, pref, unit=128):
    if n <= pref:
        return n
    t = pref // unit * unit
    while n % t:
        t -= unit
    assert t > 0, (n, pref)
    return t


def _dot(a, b):
    return jnp.dot(a, b, preferred_element_type=F32)


def _dot_nt(a, b):
    return lax.dot_general(a, b, (((1,), (1,)), ((), ())), preferred_element_type=F32)


def _dot_tn(a, b):
    return lax.dot_general(a, b, (((0,), (0,)), ((), ())), preferred_element_type=F32)


def _split3(x):
    hi = x.astype(BF16)
    r = x - hi.astype(F32)
    mid = r.astype(BF16)
    lo = (r - mid.astype(F32)).astype(BF16)
    return hi, mid, lo


def _softplus(x):
    return jnp.maximum(x, 0.0) + jnp.log1p(jnp.exp(-jnp.abs(x)))


def _sigmoid(x):
    return 1.0 / (1.0 + jnp.exp(-x))


def _silu(x):
    return x * _sigmoid(x)


def _gelu(x):
    c = math.sqrt(2.0 / math.pi)
    return x * (0.5 * (1.0 + jnp.tanh(c * (x + 0.044715 * (x * x * x)))))


def _cast_kernel(w_ref, o_ref):
    o_ref[...] = w_ref[...].astype(o_ref.dtype)


def _to_bf16(w, layer):
    _, rows, cols = w.shape
    tr = _tile(rows, max(16, 4 * MIB // (cols * 4)), unit=16)
    return pl.pallas_call(
        _cast_kernel,
        grid=(rows // tr,),
        in_specs=[pl.BlockSpec((None, tr, cols), lambda i: (layer, i, 0))],
        out_specs=pl.BlockSpec((tr, cols), lambda i: (i, 0)),
        out_shape=jax.ShapeDtypeStruct((rows, cols), BF16),
        compiler_params=_cparams(1, 2 * tr * cols * 6),
        name="weights_to_bf16",
    )(w)


def _side_cast_plan(side, n_inner, n_steps):
    in_specs, out_specs, out_shapes, vmem = [], [], [], 0
    for w, layer, transposed in side:
        step = lambda j, i, last: jnp.minimum(j * n_inner + i, last)
        if transposed:
            _, cols, rows = w.shape
            tc = HEAD_DIM * pl.cdiv(pl.cdiv(cols, HEAD_DIM), n_steps)
            last = pl.cdiv(cols, tc) - 1
            in_specs.append(pl.BlockSpec((None, tc, rows), functools.partial(
                lambda j, i, layer, last: (layer, step(j, i, last), 0), layer=layer, last=last)))
            out_specs.append(pl.BlockSpec((rows, tc), functools.partial(
                lambda j, i, last: (0, step(j, i, last)), last=last)))
            vmem += 2 * tc * rows * 6 + tc * rows * 4
        else:
            _, rows, cols = w.shape
            tr = 16
            while rows % tr or rows // tr > n_steps:
                tr += 16
            last = rows // tr - 1
            in_specs.append(pl.BlockSpec((None, tr, cols), functools.partial(
                lambda j, i, layer, last: (layer, step(j, i, last), 0), layer=layer, last=last)))
            out_specs.append(pl.BlockSpec((tr, cols), functools.partial(
                lambda j, i, last: (step(j, i, last), 0), last=last)))
            vmem += 2 * tr * cols * 6
        out_shapes.append(jax.ShapeDtypeStruct((rows, cols), BF16))
    return in_specs, out_specs, out_shapes, vmem


def _side_cast(in_refs, out_refs, transposed):
    for w_ref, o_ref, flip in zip(in_refs, out_refs, transposed):
        w = w_ref[...]
        o_ref[...] = (w.T if flip else w).astype(o_ref.dtype)


def _ada_kernel(c_ref, w_ref, b_ref, o_ref):
    c = c_ref[...]
    o_ref[...] = _dot(_silu(c).astype(BF16), w_ref[...].astype(BF16)) + b_ref[...]


def _ada_project(c, ada_w, ada_b):
    bsz, d = c.shape
    n = ada_w.shape[1]
    rows = 8
    tn = _tile(n, 512)
    c_pad = jnp.zeros((rows, d), F32).at[:bsz].set(c)
    out = pl.pallas_call(
        _ada_kernel,
        grid=(n // tn,),
        in_specs=[pl.BlockSpec((rows, d), lambda j: (0, 0)),
                  pl.BlockSpec((d, tn), lambda j: (0, j)),
                  pl.BlockSpec((1, tn), lambda j: (0, j))],
        out_specs=pl.BlockSpec((rows, tn), lambda j: (0, j)),
        out_shape=jax.ShapeDtypeStruct((rows, n), F32),
        compiler_params=_cparams(1, 2 * d * tn * 4 + d * tn * 2),
        name="ada_project",
    )(c_pad, ada_w, ada_b.reshape(1, n))
    return out[:bsz]


def _norm_mod_kernel(x_ref, g_ref, sh_ref, sc_ref, lsh_ref, lsc_ref, o_ref):
    x = x_ref[...]
    y = x * lax.rsqrt(jnp.mean(x * x, axis=-1, keepdims=True) + EPS) * g_ref[...]
    scale = 1.0 + (sc_ref[0] + lsc_ref[0])
    shift = sh_ref[0] + lsh_ref[0]
    o_ref[...] = (y * scale + shift).astype(o_ref.dtype)


def _norm_modulate(x, gain, mod_rows, layer_rows, layer, shift_idx, seq):
    t, d = x.shape
    tr = _tile(seq, 512)
    per_seq = seq // tr
    mod_spec = lambda idx: pl.BlockSpec((1, 1, d), lambda i: ((i // per_seq) * N_MOD + idx, 0, 0))
    lay_spec = lambda idx: pl.BlockSpec((1, 1, d), lambda i: (layer * N_MOD + idx, 0, 0))
    return pl.pallas_call(
        _norm_mod_kernel,
        grid=(t // tr,),
        in_specs=[pl.BlockSpec((tr, d), lambda i: (i, 0)),
                  pl.BlockSpec((1, d), lambda i: (0, 0)),
                  mod_spec(shift_idx), mod_spec(shift_idx + 1),
                  lay_spec(shift_idx), lay_spec(shift_idx + 1)],
        out_specs=pl.BlockSpec((tr, d), lambda i: (i, 0)),
        out_shape=jax.ShapeDtypeStruct((t, d), BF16),
        compiler_params=_cparams(1, 2 * tr * d * 6),
        name="norm_modulate",
    )(x, gain.reshape(1, d), mod_rows, mod_rows, layer_rows, layer_rows)


def _final_norm_kernel(x_ref, g_ref, o_ref):
    x = x_ref[...]
    o_ref[...] = x * lax.rsqrt(jnp.mean(x * x, axis=-1, keepdims=True) + EPS) * g_ref[...]


def _final_norm(x, gain):
    t, d = x.shape
    tr = _tile(t, 256)
    return pl.pallas_call(
        _final_norm_kernel,
        grid=(t // tr,),
        in_specs=[pl.BlockSpec((tr, d), lambda i: (i, 0)),
                  pl.BlockSpec((1, d), lambda i: (0, 0))],
        out_specs=pl.BlockSpec((tr, d), lambda i: (i, 0)),
        out_shape=jax.ShapeDtypeStruct((t, d), F32),
        compiler_params=_cparams(1, 2 * tr * d * 8),
        name="final_norm",
    )(x, gain.reshape(1, d))


def _mm_plain_kernel(*refs, side_t):
    n_side = len(side_t)
    a_ref, w_ref = refs[:2]
    o_ref = refs[2 + n_side]
    _side_cast(refs[2:2 + n_side], refs[3 + n_side:], side_t)
    o_ref[...] = _dot(a_ref[...], w_ref[...]).astype(o_ref.dtype)


def _matmul(a, w, col_block0, n_out, out_dtype, tm, tn, name, side=()):
    m, k = a.shape
    out_bytes = jnp.dtype(out_dtype).itemsize
    grid = (n_out // tn, m // tm)
    side_in, side_out, side_shapes, side_vmem = _side_cast_plan(side, grid[1], grid[0] * grid[1])
    vmem = 2 * k * tn * 2 + 2 * tm * k * 2 + 2 * tm * tn * out_bytes + tm * tn * 4 + side_vmem
    out, *cast = pl.pallas_call(
        functools.partial(_mm_plain_kernel, side_t=tuple(s[2] for s in side)),
        grid=grid,
        in_specs=[pl.BlockSpec((tm, k), lambda j, i: (i, 0)),
                  pl.BlockSpec((k, tn), lambda j, i: (0, j + col_block0))] + side_in,
        out_specs=[pl.BlockSpec((tm, tn), lambda j, i: (i, j))] + side_out,
        out_shape=[jax.ShapeDtypeStruct((m, n_out), out_dtype)] + side_shapes,
        compiler_params=_cparams(2, vmem),
        name=name,
    )(a, w, *[s[0] for s in side])
    return out, cast


def _mm_resid_kernel(*refs, n_parts, side_t):
    n_side = len(side_t)
    a_refs = refs[:n_parts]
    w_refs = refs[n_parts:2 * n_parts]
    x_ref, gate_ref, lgate_ref = refs[2 * n_parts:2 * n_parts + 3]
    side_in = refs[2 * n_parts + 3:2 * n_parts + 3 + n_side]
    o_ref = refs[2 * n_parts + 3 + n_side]
    side_out = refs[2 * n_parts + 4 + n_side:]
    _side_cast(side_in, side_out, side_t)
    y = _dot(a_refs[0][...], w_refs[0][...])
    for a_ref, w_ref in zip(a_refs[1:], w_refs[1:]):
        y = y + _dot(a_ref[...], w_ref[...])
    o_ref[...] = x_ref[...] + (gate_ref[0] + lgate_ref[0]) * y


def _matmul_residual(a_parts, w, x, mod_rows, layer_rows, layer, gate_idx, seq, tm, tn, name, side=()):
    n_parts = len(a_parts)
    m, kp = a_parts[0].shape
    n = w.shape[1]
    per_seq = seq // tm
    grid = (n // tn, m // tm)
    side_in, side_out, side_shapes, side_vmem = _side_cast_plan(side, grid[1], grid[0] * grid[1])
    vmem = n_parts * (2 * kp * tn * 2 + 2 * tm * kp * 2) + 5 * tm * tn * 4 + side_vmem
    a_specs = [pl.BlockSpec((tm, kp), lambda j, i: (i, 0)) for _ in range(n_parts)]
    w_specs = [pl.BlockSpec((kp, tn), functools.partial(lambda j, i, p: (p, j), p=p))
               for p in range(n_parts)]
    out, *cast = pl.pallas_call(
        functools.partial(_mm_resid_kernel, n_parts=n_parts, side_t=tuple(s[2] for s in side)),
        grid=grid,
        in_specs=a_specs + w_specs + [
            pl.BlockSpec((tm, tn), lambda j, i: (i, j)),
            pl.BlockSpec((1, 1, tn), lambda j, i: ((i // per_seq) * N_MOD + gate_idx, 0, j)),
            pl.BlockSpec((1, 1, tn), lambda j, i: (layer * N_MOD + gate_idx, 0, j))] + side_in,
        out_specs=[pl.BlockSpec((tm, tn), lambda j, i: (i, j))] + side_out,
        out_shape=[jax.ShapeDtypeStruct((m, n), F32)] + side_shapes,
        compiler_params=_cparams(2, vmem),
        name=name,
    )(*a_parts, *([w] * n_parts), x, mod_rows, layer_rows, *[s[0] for s in side])
    return out, cast


HALO = 8
FFN_SUB_ROWS = 256
GDN_SUB_ROWS = 128


def _conv_begin(ybuf_ref, tm, tiles_per_seq):
    starts_sequence = (pl.program_id(1) % tiles_per_seq) == 0

    @pl.when(starts_sequence)
    def _():
        ybuf_ref[0:HALO, :] = jnp.zeros((HALO, ybuf_ref.shape[1]), F32)

    @pl.when(jnp.logical_not(starts_sequence))
    def _():
        ybuf_ref[0:HALO, :] = ybuf_ref[tm:tm + HALO, :]


def _conv_rows(y, ybuf_ref, cw_ref, r0):
    rows = y.shape[0]
    taps = cw_ref.shape[0]
    ybuf_ref[HALO + r0:HALO + r0 + rows, :] = y
    out = cw_ref[taps - 1:taps, :] * y
    for j in range(taps - 1):
        off = HALO + r0 - (taps - 1) + j
        out = out + cw_ref[j:j + 1, :] * ybuf_ref[off:off + rows, :]
    return out


def _mm_ffn_up_kernel(*refs, tiles_per_seq, sub, side_t):
    n_side = len(side_t)
    a_ref, wg_ref, wv_ref, cw_ref, cb_ref = refs[:5]
    side_in = refs[5:5 + n_side]
    o_ref = refs[5 + n_side]
    side_out = refs[6 + n_side:6 + 2 * n_side]
    ybuf_ref = refs[6 + 2 * n_side]
    _side_cast(side_in, side_out, side_t)
    tm = a_ref.shape[0]
    _conv_begin(ybuf_ref, tm, tiles_per_seq)
    for r0 in range(0, tm, sub):
        a = a_ref[r0:r0 + sub, :]
        gate = _conv_rows(_dot(a, wg_ref[...]), ybuf_ref, cw_ref, r0) + cb_ref[...]
        o_ref[r0:r0 + sub, :] = (_gelu(gate) * _dot(a, wv_ref[...])).astype(o_ref.dtype)


def _ffn_up(h, w_up, conv_w, conv_b, seq, tm, tn, side=()):
    m, k = h.shape
    d_ff = w_up.shape[1] // 2
    nb = d_ff // tn
    sub = _tile(tm, FFN_SUB_ROWS, 8)
    grid = (nb, m // tm)
    side_in, side_out, side_shapes, side_vmem = _side_cast_plan(side, grid[1], grid[0] * grid[1])
    vmem = 2 * 2 * k * tn * 2 + 2 * tm * k * 2 + 3 * tm * tn * 4 + 8 * sub * tn * 4 + side_vmem
    out, *cast = pl.pallas_call(
        functools.partial(_mm_ffn_up_kernel, tiles_per_seq=seq // tm, sub=sub,
                          side_t=tuple(s[2] for s in side)),
        grid=grid,
        in_specs=[pl.BlockSpec((tm, k), lambda j, i: (i, 0)),
                  pl.BlockSpec((k, tn), lambda j, i: (0, j)),
                  pl.BlockSpec((k, tn), lambda j, i: (0, j + nb)),
                  pl.BlockSpec((conv_w.shape[0], tn), lambda j, i: (0, j)),
                  pl.BlockSpec((1, tn), lambda j, i: (0, j))] + side_in,
        out_specs=[pl.BlockSpec((tm, tn), lambda j, i: (i, j))] + side_out,
        out_shape=[jax.ShapeDtypeStruct((m, d_ff), BF16)] + side_shapes,
        scratch_shapes=[pltpu.VMEM((tm + HALO, tn), F32)],
        compiler_params=_cparams(2, vmem),
        name="ffn_up",
    )(h, w_up, w_up, conv_w, conv_b.reshape(1, d_ff), *[s[0] for s in side])
    return out, cast


def _mm_gdn_proj_kernel(*refs, kinds, tiles_per_seq, sub):
    n = len(kinds)
    n_conv = sum(kind[0] != "plain" for kind in kinds)
    a_ref = refs[0]
    w_refs = refs[1:1 + n]
    cw_refs = iter(refs[1 + n:1 + n + n_conv])
    o_refs = refs[1 + n + n_conv:1 + 2 * n + n_conv]
    ybuf_refs = iter(refs[1 + 2 * n + n_conv:])
    conv_refs = [None if kind[0] == "plain" else (next(cw_refs), next(ybuf_refs)) for kind in kinds]
    tm = a_ref.shape[0]
    for conv in conv_refs:
        if conv is not None:
            _conv_begin(conv[1], tm, tiles_per_seq)
    for r0 in range(0, tm, sub):
        a = a_ref[r0:r0 + sub, :]
        for kind, w_ref, conv, o_ref in zip(kinds, w_refs, conv_refs, o_refs):
            y = _dot(a, w_ref[...])
            if conv is not None:
                y = _silu(_conv_rows(y, conv[1], conv[0], r0))
            if kind[0] != "l2":
                o_ref[r0:r0 + sub, :] = y.astype(o_ref.dtype)
                continue
            for g in range(y.shape[1] // HEAD_DIM):
                cols = slice(g * HEAD_DIM, (g + 1) * HEAD_DIM)
                yg = y[:, cols]
                inv = lax.rsqrt(jnp.sum(yg * yg, axis=-1, keepdims=True) + EPS)
                o_ref[r0:r0 + sub, cols] = (yg * (inv * kind[1])).astype(o_ref.dtype)


def _gdn_proj(h, w_in, conv_w, streams, n_out, seq, tm, tn, name):
    m, k = h.shape
    sub = _tile(tm, GDN_SUB_ROWS, 8)
    n = len(streams)
    conv_streams = [s for s in streams if s[1][0] != "plain"]
    vmem = n * (2 * k * tn * 2 + 3 * tm * tn * 4 + 8 * sub * tn * 4) + 2 * tm * k * 2
    w_specs = [pl.BlockSpec((k, tn), functools.partial(lambda j, i, c0: (0, j + c0), c0=s[0]))
               for s in streams]
    cw_specs = [pl.BlockSpec((conv_w.shape[0], tn), functools.partial(lambda j, i, c0: (0, j + c0), c0=s[0]))
                for s in conv_streams]
    out_spec = pl.BlockSpec((tm, tn), lambda j, i: (i, j))
    return pl.pallas_call(
        functools.partial(_mm_gdn_proj_kernel, kinds=tuple(s[1] for s in streams),
                          tiles_per_seq=seq // tm, sub=sub),
        grid=(n_out // tn, m // tm),
        in_specs=[pl.BlockSpec((tm, k), lambda j, i: (i, 0))] + w_specs + cw_specs,
        out_specs=[out_spec] * n,
        out_shape=[jax.ShapeDtypeStruct((m, n_out), s[2]) for s in streams],
        scratch_shapes=[pltpu.VMEM((tm + HALO, tn), F32) for _ in conv_streams],
        compiler_params=_cparams(2, vmem),
        name=name,
    )(h, *([w_in] * n), *([conv_w] * len(conv_streams)))


def _chunk_masks(blk, ch):
    shift = ch.bit_length() - 1
    r = lax.broadcasted_iota(jnp.int32, (blk, blk), 0)
    c = lax.broadcasted_iota(jnp.int32, (blk, blk), 1)
    same_chunk = jnp.right_shift(r, shift) == jnp.right_shift(c, shift)
    return (jnp.logical_and(same_chunk, r >= c), jnp.logical_and(same_chunk, r > c), same_chunk)


def _mm_gates_kernel(a_ref, w_ref, alog_ref, dt_ref, o_ref, *, n_heads):
    tm = a_ref.shape[0]
    blk = GDN_BLOCK
    lower_incl, _, same_chunk = _chunk_masks(blk, GDN_CHUNK)
    cum_mat = lower_incl.astype(BF16)
    tot_mat = same_chunk.astype(BF16)
    lane = lax.broadcasted_iota(jnp.int32, (blk, HEAD_DIM), 1)
    w_ab = w_ref[...]
    w = jnp.concatenate([w_ab, w_ab[:n_heads],
                         jnp.zeros((HEAD_DIM - 3 * n_heads, w_ab.shape[1]), F32)], axis=0).astype(BF16)
    for r0 in range(0, tm, blk):
        y = _dot_nt(a_ref[r0:r0 + blk, :], w)
        g = -jnp.exp(alog_ref[...]) * _softplus(y + dt_ref[...])
        hi, mid, lo = _split3(g)
        cum = _dot(cum_mat, hi) + _dot(cum_mat, mid) + _dot(cum_mat, lo)
        tot = _dot(tot_mat, hi) + _dot(tot_mat, mid) + _dot(tot_mat, lo)
        o_ref[r0:r0 + blk, :] = jnp.where(lane < n_heads, cum,
                                          jnp.where(lane < 2 * n_heads, _sigmoid(y), tot))


def _gdn_gates(h, w_in_t, layer, a_log, dt_bias, tm):
    m, k = h.shape
    n_heads = a_log.shape[0]
    tn = HEAD_DIM
    n_rows = w_in_t.shape[1]
    assert 3 * n_heads <= tn and tm % GDN_BLOCK == 0 and n_rows % (2 * n_heads) == 0
    zeros_h = jnp.zeros((n_heads,), F32)
    pad = lambda v: jnp.concatenate([v, zeros_h, v, jnp.zeros((tn - 3 * n_heads,), F32)]).reshape(1, tn)
    vmem = 4 * tn * k * 4 + 2 * tm * k * 2 + 4 * tm * tn * 4
    return pl.pallas_call(
        functools.partial(_mm_gates_kernel, n_heads=n_heads),
        grid=(m // tm,),
        in_specs=[pl.BlockSpec((tm, k), lambda i: (i, 0)),
                  pl.BlockSpec((None, 2 * n_heads, k), lambda i: (layer, n_rows // (2 * n_heads) - 1, 0)),
                  pl.BlockSpec((1, tn), lambda i: (0, 0)),
                  pl.BlockSpec((1, tn), lambda i: (0, 0))],
        out_specs=pl.BlockSpec((tm, tn), lambda i: (i, 0)),
        out_shape=jax.ShapeDtypeStruct((m, tn), F32),
        compiler_params=_cparams(1, vmem),
        name="gdn_gates",
    )(h, w_in_t, pad(a_log), pad(dt_bias))


def _sb_kernel(q_ref, k_ref, v_ref, o_ref, *, tq, n_pack, scale):
    qi = pl.program_id(2)
    row = lax.broadcasted_iota(jnp.int32, (tq, tq), 0)
    col = lax.broadcasted_iota(jnp.int32, (tq, tq), 1)
    later = (row > col).astype(BF16)
    below_diag = col < row
    head_cols = [slice(p * HEAD_DIM, (p + 1) * HEAD_DIM) for p in range(n_pack)]
    qs = [q_ref[:, cols] for cols in head_cols]

    def key_tile(kb, cs, accs, on_diagonal):
        start = pl.multiple_of(kb * tq, tq)
        keep = (lambda x: jnp.where(below_diag, x, 0.0)) if on_diagonal else (lambda x: x)
        heads = range(n_pack)
        z = [_dot_nt(qs[p], k_ref[pl.ds(start, tq), head_cols[p]]) * scale for p in heads]
        sp = [jnp.maximum(z[p], 0.0) + jnp.log(1.0 + jnp.exp(-jnp.abs(z[p]))) for p in heads]
        log_beta = [z[p] - sp[p] for p in heads]
        log_1m = [keep(-sp[p]) for p in heads]
        tail = [_dot(log_1m[p].astype(BF16), later) + cs[p] for p in heads]
        w = [keep(jnp.exp(log_beta[p] + tail[p])).astype(BF16) for p in heads]
        new_accs = [accs[p] + _dot(w[p], v_ref[pl.ds(start, tq), head_cols[p]]) for p in heads]
        new_cs = [cs[p] + jnp.sum(log_1m[p], axis=1, keepdims=True) for p in heads]
        c_max = functools.reduce(jnp.maximum, [jnp.max(c) for c in new_cs])
        return kb - 1, tuple(new_cs), tuple(new_accs), c_max > -F32_EXP_UNDERFLOW

    def cond(carry):
        kb, _, _, live = carry
        return jnp.logical_and(kb >= 0, live)

    zeros = lambda width: tuple(jnp.zeros((tq, width), F32) for _ in range(n_pack))
    carry = key_tile(qi, zeros(1), zeros(HEAD_DIM), True)
    _, _, accs, _ = lax.while_loop(cond, lambda c: key_tile(c[0], c[1], c[2], False), carry)
    for cols, acc in zip(head_cols, accs):
        o_ref[:, cols] = acc.astype(o_ref.dtype)


def _stick_breaking(proj, bsz, seq, n_heads, tq):
    t = proj.shape[0]
    nq = seq // tq
    n_pack = SB_HEAD_PACK if n_heads % SB_HEAD_PACK == 0 else 1
    groups = n_heads // n_pack
    width = n_pack * HEAD_DIM
    vmem = 2 * 2 * seq * width * 2 + 4 * tq * width * 2 + n_pack * 12 * tq * tq * 4
    return pl.pallas_call(
        functools.partial(_sb_kernel, tq=tq, n_pack=n_pack, scale=HEAD_DIM ** -0.5),
        grid=(bsz, groups, nq),
        in_specs=[pl.BlockSpec((tq, width), lambda b, h, i: (b * nq + i, h)),
                  pl.BlockSpec((seq, width), lambda b, h, i: (b, groups + h)),
                  pl.BlockSpec((seq, width), lambda b, h, i: (b, 2 * groups + h))],
        out_specs=pl.BlockSpec((tq, width), lambda b, h, i: (b * nq + i, h)),
        out_shape=jax.ShapeDtypeStruct((t, n_heads * HEAD_DIM), BF16),
        compiler_params=_cparams(3, vmem),
        name="stick_breaking",
    )(proj, proj, proj)


def _sgu_kernel(u_ref, v_ref, gain_ref, w_ref, b_ref, o_ref):
    rows = u_ref.shape[0]
    r = lax.broadcasted_iota(jnp.int32, (SGU_LEN, SGU_LEN), 0)
    c = lax.broadcasted_iota(jnp.int32, (SGU_LEN, SGU_LEN), 1)
    w = jnp.where(r >= c, w_ref[0], 0.0).astype(BF16)
    bias = b_ref[0]
    gain = gain_ref[0]
    for n in range(rows // SGU_LEN):
        sl = slice(n * SGU_LEN, (n + 1) * SGU_LEN)
        u = _gelu(u_ref[sl, :].astype(F32))
        v = _gelu(v_ref[sl, :].astype(F32))
        v = v * lax.rsqrt(jnp.mean(v * v, axis=-1, keepdims=True) + EPS) * gain
        mixed = _dot(w, v.astype(BF16)) + bias
        o_ref[sl, :] = (u * mixed).astype(o_ref.dtype)


def _spatial_gating(proj, sgu_gain, sgu_w, sgu_b, n_groups, u_block0, rows):
    t = proj.shape[0]
    return pl.pallas_call(
        _sgu_kernel,
        grid=(n_groups, t // rows),
        in_specs=[pl.BlockSpec((rows, HEAD_DIM), lambda g, i: (i, u_block0 + g)),
                  pl.BlockSpec((rows, HEAD_DIM), lambda g, i: (i, u_block0 + n_groups + g)),
                  pl.BlockSpec((1, 1, HEAD_DIM), lambda g, i: (g, 0, 0)),
                  pl.BlockSpec((1, SGU_LEN, SGU_LEN), lambda g, i: (g, 0, 0)),
                  pl.BlockSpec((1, SGU_LEN, 1), lambda g, i: (g, 0, 0))],
        out_specs=pl.BlockSpec((rows, HEAD_DIM), lambda g, i: (i, g)),
        out_shape=jax.ShapeDtypeStruct((t, n_groups * HEAD_DIM), BF16),
        compiler_params=_cparams(2, 16 * rows * HEAD_DIM * 4),
        name="spatial_gating",
    )(proj, proj, sgu_gain.reshape(n_groups, 1, HEAD_DIM), sgu_w,
      sgu_b.reshape(n_groups, SGU_LEN, 1))


def _neumann_solve(p_mats, xs, n_factors):
    idx = range(len(xs))
    n = p_mats[0].shape[0]
    eye = (lax.broadcasted_iota(jnp.int32, (n, n), 0)
           == lax.broadcasted_iota(jnp.int32, (n, n), 1)).astype(F32)
    ts = [eye + p_mats[i] for i in idx]
    for _ in range(1, n_factors):
        pb = [p_mats[i].astype(BF16) for i in idx]
        p_mats = [_dot(pb[i], pb[i]) for i in idx]
        ts = [ts[i] + _dot(p_mats[i].astype(BF16), ts[i].astype(BF16)) for i in idx]
    return [_dot(ts[i].astype(BF16), xs[i].astype(BF16)) for i in idx]


def _gdn_kernel(q_ref, k_ref, v_ref, z_ref, gb_ref, gain_ref, o_ref, state_ref, *, n_heads, n_pack):
    hg = pl.program_id(1)
    n_rows = q_ref.shape[0]
    blk, ch = GDN_BLOCK, GDN_CHUNK
    n_ch = blk // ch
    n_factors = ch.bit_length() - 1

    @pl.when(pl.program_id(2) == 0)
    def _():
        state_ref[...] = jnp.zeros(state_ref.shape, F32)

    lower_incl, lower_strict, _ = _chunk_masks(blk, ch)
    lane = lax.broadcasted_iota(jnp.int32, (blk, HEAD_DIM), 1)
    gain = gain_ref[...]

    def pick(gb, lane_idx):
        return jnp.sum(jnp.where(lane == lane_idx, gb, 0.0), axis=1, keepdims=True)

    def diff_operands(gc):
        chi, cmid, clo = (part.astype(F32) for part in _split3(jnp.broadcast_to(gc, (blk, HEAD_DIM))))
        left = jnp.where(lane == 0, chi, jnp.where(lane == 1, cmid, jnp.where(
            lane == 2, clo, jnp.where(lane < 6, 1.0, 0.0))))
        right = jnp.where(lane == 3, -chi, jnp.where(lane == 4, -cmid, jnp.where(
            lane == 5, -clo, jnp.where(lane < 3, 1.0, 0.0))))
        return left.astype(BF16), right.astype(BF16)

    head_cols = [slice(p * HEAD_DIM, (p + 1) * HEAD_DIM) for p in range(n_pack)]
    heads = range(n_pack)

    def block(b, states):
        rows = pl.ds(pl.multiple_of(b * blk, blk), blk)
        gb = gb_ref[rows, :]
        q = [q_ref[rows, cols].astype(F32) for cols in head_cols]
        k = [k_ref[rows, cols].astype(F32) for cols in head_cols]
        v = [v_ref[rows, cols].astype(F32) for cols in head_cols]
        gc = [pick(gb, hg * n_pack + p) for p in heads]
        beta = [pick(gb, hg * n_pack + p + n_heads) for p in heads]
        gl = [pick(gb, hg * n_pack + p + 2 * n_heads) for p in heads]
        diff = [_dot_nt(*diff_operands(g)) for g in gc]
        decay = [jnp.exp(jnp.where(lower_incl, d, -1e30)) for d in diff]
        eg = [jnp.exp(g) for g in gc]
        kb = [k[p] * beta[p] for p in heads]
        kbf = [k[p].astype(BF16) for p in heads]
        p_mat = [jnp.where(lower_strict, _dot_nt((-kb[p]).astype(BF16), kbf[p]) * decay[p], 0.0)
                 for p in heads]
        qk = [(_dot_nt(q[p].astype(BF16), kbf[p]) * decay[p]).astype(BF16) for p in heads]
        x = [jnp.concatenate([v[p] * beta[p], kb[p] * eg[p]], axis=1) for p in heads]
        x = _neumann_solve(p_mat, x, n_factors)
        u = [x[p][:, :HEAD_DIM] for p in heads]
        w = [x[p][:, HEAD_DIM:].astype(BF16) for p in heads]
        q_dec = [(q[p] * eg[p]).astype(BF16) for p in heads]
        k_dec = [(k[p] * jnp.exp(gl[p] - gc[p])).astype(BF16) for p in heads]
        g_tot = [jnp.exp(gl[p]) for p in heads]

        states = list(states)
        o_inter = [[] for _ in heads]
        v_new = [[] for _ in heads]
        for ci in range(n_ch):
            sl = slice(ci * ch, (ci + 1) * ch)
            sb = [states[p].astype(BF16) for p in heads]
            v_c = [u[p][sl] - _dot(w[p][sl], sb[p]) for p in heads]
            for p in heads:
                o_inter[p].append(_dot(q_dec[p][sl], sb[p]))
                v_new[p].append(v_c[p])
            states = [states[p] * g_tot[p][ci * ch:ci * ch + 1, :]
                      + _dot_tn(k_dec[p][sl], v_c[p].astype(BF16)) for p in heads]
        o = [jnp.concatenate(o_inter[p], axis=0)
             + _dot(qk[p], jnp.concatenate(v_new[p], axis=0).astype(BF16)) for p in heads]
        for p, cols in enumerate(head_cols):
            on = o[p] * lax.rsqrt(jnp.mean(o[p] * o[p], axis=-1, keepdims=True) + EPS) * gain
            o_ref[rows, cols] = (on * _silu(z_ref[rows, cols].astype(F32))).astype(o_ref.dtype)
        return tuple(states)

    states = lax.fori_loop(0, n_rows // blk, block, tuple(state_ref[p] for p in range(n_pack)),
                           unroll=4)
    for p in range(n_pack):
        state_ref[p] = states[p]


def _gated_delta(q, k, v, z, gates, o_gain, bsz, seq, n_heads):
    t = q.shape[0]
    n_pack = GDN_HEAD_PACK if n_heads % GDN_HEAD_PACK == 0 else 1
    width = n_pack * HEAD_DIM
    rows = _tile(seq, 2048, GDN_BLOCK)
    tiles = seq // rows
    head_spec = pl.BlockSpec((rows, width), lambda b, h, r: (b * tiles + r, h))
    io_bytes = sum(jnp.dtype(a.dtype).itemsize for a in (q, k, v, z)) + 2
    vmem = 2 * (rows * width * io_bytes + rows * HEAD_DIM * 4) + n_pack * 40 * GDN_BLOCK * GDN_BLOCK * 4
    return pl.pallas_call(
        functools.partial(_gdn_kernel, n_heads=n_heads, n_pack=n_pack),
        grid=(bsz, n_heads // n_pack, tiles),
        in_specs=[head_spec, head_spec, head_spec, head_spec,
                  pl.BlockSpec((rows, HEAD_DIM), lambda b, h, r: (b * tiles + r, 0)),
                  pl.BlockSpec((1, HEAD_DIM), lambda b, h, r: (0, 0))],
        out_specs=head_spec,
        out_shape=jax.ShapeDtypeStruct((t, n_heads * HEAD_DIM), BF16),
        scratch_shapes=[pltpu.VMEM((n_pack, HEAD_DIM, HEAD_DIM), F32)],
        compiler_params=_cparams(3, vmem),
        name="gated_delta",
    )(q, k, v, z, gates, o_gain.reshape(1, HEAD_DIM))


def kernel(x, c, ada_w, ada_b, ada_layer, norm_mix, norm_ffn, norm_final, ev_w_in, ev_w_out, sgu_gain, sgu_w, sgu_b, gdn_w_in, gdn_conv, gdn_a_log, gdn_dt_bias, gdn_o_gain, gdn_w_out, ffn_w_up, ffn_conv, ffn_conv_b, ffn_w_down):
    bsz, seq, d = x.shape
    depth = ada_layer.shape[0]
    t = bsz * seq
    d_ff = ffn_w_down.shape[1]
    n_sb = d // 2 // HEAD_DIM
    n_sgu = sgu_gain.shape[1]
    n_gdn = gdn_a_log.shape[1]
    assert sgu_gain.shape[2] == HEAD_DIM and d == n_gdn * HEAD_DIM and sgu_w.shape[2] == SGU_LEN
    assert n_sgu == n_sb and seq % GDN_BLOCK == 0
    tm = _tile(seq, 1024)

    gdn_w_in_t = jnp.swapaxes(gdn_w_in, 1, 2)

    def mixer_weights(layer):
        if layer % 2 == 0:
            return (ev_w_in, layer // 2, False), (ev_w_out, layer // 2, False)
        return (gdn_w_in_t, layer // 2, True), (gdn_w_out, layer // 2, False)

    assert depth > 0
    w_mix_in, w_mix_out, w_up = _to_bf16(ev_w_in, 0), _to_bf16(ev_w_out, 0), _to_bf16(ffn_w_up, 0)

    mod_rows = _ada_project(c, ada_w, ada_b).reshape(bsz * N_MOD, 1, d)
    layer_rows = ada_layer.reshape(depth * N_MOD, 1, d)
    x = x.reshape(t, d)

    for layer in range(depth):
        h = _norm_modulate(x, norm_mix[layer], mod_rows, layer_rows, layer, 0, seq)
        if layer % 2 == 0:
            e = layer // 2
            n_proj = 5 * (d // 2)
            proj, early_cast = _matmul(h, w_mix_in, 0, n_proj, BF16, tm, _tile(n_proj, 1024),
                                       "even_in_proj",
                                       side=[mixer_weights(layer + 1)[0]] if layer + 1 < depth else [])
            o_a = _stick_breaking(proj, bsz, seq, n_sb, _tile(seq, 256))
            o_b = _spatial_gating(proj, sgu_gain[e], sgu_w[e], sgu_b[e], n_sgu, 3 * n_sb,
                                  _tile(seq, 2048))
            x, _ = _matmul_residual([o_a, o_b], w_mix_out, x, mod_rows, layer_rows, layer, 2, seq,
                                    tm, _tile(d, 512), "even_out_proj")
        else:
            o = layer // 2
            tn = _tile(d, 512)
            nb = d // tn
            conv_w = gdn_conv[o]
            q, k = _gdn_proj(h, w_mix_in, conv_w,
                             [(0, ("l2", HEAD_DIM ** -0.5), BF16), (nb, ("l2", 1.0), BF16)],
                             d, seq, tm, tn, "gdn_qk_proj")
            v, z = _gdn_proj(h, w_mix_in, conv_w,
                             [(2 * nb, ("silu",), F32), (3 * nb, ("plain",), F32)],
                             d, seq, tm, tn, "gdn_vz_proj")
            gates = _gdn_gates(h, gdn_w_in_t, o, gdn_a_log[o], gdn_dt_bias[o], tm)
            y = _gated_delta(q, k, v, z, gates, gdn_o_gain[o], bsz, seq, n_gdn)
            x, _ = _matmul_residual([y], w_mix_out, x, mod_rows, layer_rows, layer, 2, seq,
                                    tm, _tile(d, 512), "gdn_out_proj")
        has_next = layer + 1 < depth
        side = [(ffn_w_down, layer, False)]
        if has_next:
            next_in, next_out = mixer_weights(layer + 1)
            side += ([] if layer % 2 == 0 else [next_in]) + [next_out, (ffn_w_up, layer + 1, False)]
        h = _norm_modulate(x, norm_ffn[layer], mod_rows, layer_rows, layer, 3, seq)
        f, cast = _ffn_up(h, w_up, ffn_conv[layer], ffn_conv_b[layer], seq, tm, _tile(d_ff, 512),
                          side=side)
        w_down = cast[0]
        if has_next:
            w_mix_in = early_cast[0] if layer % 2 == 0 else cast[1]
            w_mix_out, w_up = cast[-2:]
        x, _ = _matmul_residual([f], w_down, x, mod_rows, layer_rows, layer, 5, seq,
                                _tile(seq, 512), _tile(d, 512), "ffn_down_proj")
    return _final_norm(x, norm_final).reshape(bsz, seq, d)
```

```python
import functools
import math

import jax
import jax.numpy as jnp
from jax import lax
from jax.experimental import pallas as pl
from jax.experimental.pallas import tpu as pltpu

F32 = jnp.float32
BF16 = jnp.bfloat16

HEAD_DIM = 128
SGU_LEN = 128
GDN_CHUNK = 64
GDN_BLOCK = 256
SB_HEAD_PACK = 4
GDN_HEAD_PACK = 4
N_MOD = 6
EPS = 1e-6
F32_EXP_UNDERFLOW = 104.0
V7X_VMEM_BYTES = 64 * 1024 * 1024
VMEM_CAP = V7X_VMEM_BYTES - 6 * 1024 * 1024
MIB = 1024 * 1024


def _cparams(n_axes, vmem_bytes):
    limit = int(min(VMEM_CAP, max(32 * MIB, vmem_bytes * 5 // 4 + 4 * MIB)))
    return pltpu.CompilerParams(dimension_semantics=("arbitrary",) * n_axes,
                                vmem_limit_bytes=limit)


def _tile(n, pref, unit=128):
    if n <= pref:
        return n
    t = pref // unit * unit
    while n % t:
        t -= unit
    assert t > 0, (n, pref)
    return t


def _dot(a, b):
    return jnp.dot(a, b, preferred_element_type=F32)


def _dot_nt(a, b):
    return lax.dot_general(a, b, (((1,), (1,)), ((), ())), preferred_element_type=F32)


def _dot_tn(a, b):
    return lax.dot_general(a, b, (((0,), (0,)), ((), ())), preferred_element_type=F32)


def _split3(x):
    hi = x.astype(BF16)
    r = x - hi.astype(F32)
    mid = r.astype(BF16)
    lo = (r - mid.astype(F32)).astype(BF16)
    return hi, mid, lo


def _softplus(x):
    return jnp.maximum(x, 0.0) + jnp.log1p(jnp.exp(-jnp.abs(x)))


def _sigmoid(x):
    return 1.0 / (1.0 + jnp.exp(-x))


def _silu(x):
    return x * _sigmoid(x)


def _gelu(x):
    c = math.sqrt(2.0 / math.pi)
    return x * (0.5 * (1.0 + jnp.tanh(c * (x + 0.044715 * (x * x * x)))))


def _cast_kernel(w_ref, o_ref):
    o_ref[...] = w_ref[...].astype(o_ref.dtype)


def _to_bf16(w, layer):
    _, rows, cols = w.shape
    tr = _tile(rows, max(16, 4 * MIB // (cols * 4)), unit=16)
    return pl.pallas_call(
        _cast_kernel,
        grid=(rows // tr,),
        in_specs=[pl.BlockSpec((None, tr, cols), lambda i: (layer, i, 0))],
        out_specs=pl.BlockSpec((tr, cols), lambda i: (i, 0)),
        out_shape=jax.ShapeDtypeStruct((rows, cols), BF16),
        compiler_params=_cparams(1, 2 * tr * cols * 6),
        name="weights_to_bf16",
    )(w)


def _side_cast_plan(side, n_inner, n_steps):
    in_specs, out_specs, out_shapes, vmem = [], [], [], 0
    for w, layer, transposed in side:
        step = lambda j, i, last: jnp.minimum(j * n_inner + i, last)
        if transposed:
            _, cols, rows = w.shape
            tc = HEAD_DIM * pl.cdiv(pl.cdiv(cols, HEAD_DIM), n_steps)
            last = pl.cdiv(cols, tc) - 1
            in_specs.append(pl.BlockSpec((None, tc, rows), functools.partial(
                lambda j, i, layer, last: (layer, step(j, i, last), 0), layer=layer, last=last)))
            out_specs.append(pl.BlockSpec((rows, tc), functools.partial(
                lambda j, i, last: (0, step(j, i, last)), last=last)))
            vmem += 2 * tc * rows * 6 + tc * rows * 4
        else:
            _, rows, cols = w.shape
            tr = 16
            while rows % tr or rows // tr > n_steps:
                tr += 16
            last = rows // tr - 1
            in_specs.append(pl.BlockSpec((None, tr, cols), functools.partial(
                lambda j, i, layer, last: (layer, step(j, i, last), 0), layer=layer, last=last)))
            out_specs.append(pl.BlockSpec((tr, cols), functools.partial(
                lambda j, i, last: (step(j, i, last), 0), last=last)))
            vmem += 2 * tr * cols * 6
        out_shapes.append(jax.ShapeDtypeStruct((rows, cols), BF16))
    return in_specs, out_specs, out_shapes, vmem


def _side_cast(in_refs, out_refs, transposed):
    for w_ref, o_ref, flip in zip(in_refs, out_refs, transposed):
        w = w_ref[...]
        o_ref[...] = (w.T if flip else w).astype(o_ref.dtype)


def _ada_kernel(c_ref, w_ref, b_ref, o_ref):
    c = c_ref[...]
    o_ref[...] = _dot(_silu(c).astype(BF16), w_ref[...].astype(BF16)) + b_ref[...]


def _ada_project(c, ada_w, ada_b):
    bsz, d = c.shape
    n = ada_w.shape[1]
    rows = 8
    tn = _tile(n, 512)
    c_pad = jnp.zeros((rows, d), F32).at[:bsz].set(c)
    out = pl.pallas_call(
        _ada_kernel,
        grid=(n // tn,),
        in_specs=[pl.BlockSpec((rows, d), lambda j: (0, 0)),
                  pl.BlockSpec((d, tn), lambda j: (0, j)),
                  pl.BlockSpec((1, tn), lambda j: (0, j))],
        out_specs=pl.BlockSpec((rows, tn), lambda j: (0, j)),
        out_shape=jax.ShapeDtypeStruct((rows, n), F32),
        compiler_params=_cparams(1, 2 * d * tn * 4 + d * tn * 2),
        name="ada_project",
    )(c_pad, ada_w, ada_b.reshape(1, n))
    return out[:bsz]


def _norm_mod_kernel(x_ref, g_ref, sh_ref, sc_ref, lsh_ref, lsc_ref, o_ref):
    x = x_ref[...]
    y = x * lax.rsqrt(jnp.mean(x * x, axis=-1, keepdims=True) + EPS) * g_ref[...]
    scale = 1.0 + (sc_ref[0] + lsc_ref[0])
    shift = sh_ref[0] + lsh_ref[0]
    o_ref[...] = (y * scale + shift).astype(o_ref.dtype)


def _norm_modulate(x, gain, mod_rows, layer_rows, layer, shift_idx, seq):
    t, d = x.shape
    tr = _tile(seq, 512)
    per_seq = seq // tr
    mod_spec = lambda idx: pl.BlockSpec((1, 1, d), lambda i: ((i // per_seq) * N_MOD + idx, 0, 0))
    lay_spec = lambda idx: pl.BlockSpec((1, 1, d), lambda i: (layer * N_MOD + idx, 0, 0))
    return pl.pallas_call(
        _norm_mod_kernel,
        grid=(t // tr,),
        in_specs=[pl.BlockSpec((tr, d), lambda i: (i, 0)),
                  pl.BlockSpec((1, d), lambda i: (0, 0)),
                  mod_spec(shift_idx), mod_spec(shift_idx + 1),
                  lay_spec(shift_idx), lay_spec(shift_idx + 1)],
        out_specs=pl.BlockSpec((tr, d), lambda i: (i, 0)),
        out_shape=jax.ShapeDtypeStruct((t, d), BF16),
        compiler_params=_cparams(1, 2 * tr * d * 6),
        name="norm_modulate",
    )(x, gain.reshape(1, d), mod_rows, mod_rows, layer_rows, layer_rows)


def _final_norm_kernel(x_ref, g_ref, o_ref):
    x = x_ref[...]
    o_ref[...] = x * lax.rsqrt(jnp.mean(x * x, axis=-1, keepdims=True) + EPS) * g_ref[...]


def _final_norm(x, gain):
    t, d = x.shape
    tr = _tile(t, 256)
    return pl.pallas_call(
        _final_norm_kernel,
        grid=(t // tr,),
        in_specs=[pl.BlockSpec((tr, d), lambda i: (i, 0)),
                  pl.BlockSpec((1, d), lambda i: (0, 0))],
        out_specs=pl.BlockSpec((tr, d), lambda i: (i, 0)),
        out_shape=jax.ShapeDtypeStruct((t, d), F32),
        compiler_params=_cparams(1, 2 * tr * d * 8),
        name="final_norm",
    )(x, gain.reshape(1, d))


def _mm_plain_kernel(*refs, side_t):
    n_side = len(side_t)
    a_ref, w_ref = refs[:2]
    o_ref = refs[2 + n_side]
    _side_cast(refs[2:2 + n_side], refs[3 + n_side:], side_t)
    o_ref[...] = _dot(a_ref[...], w_ref[...]).astype(o_ref.dtype)


def _matmul(a, w, col_block0, n_out, out_dtype, tm, tn, name, side=()):
    m, k = a.shape
    out_bytes = jnp.dtype(out_dtype).itemsize
    grid = (n_out // tn, m // tm)
    side_in, side_out, side_shapes, side_vmem = _side_cast_plan(side, grid[1], grid[0] * grid[1])
    vmem = 2 * k * tn * 2 + 2 * tm * k * 2 + 2 * tm * tn * out_bytes + tm * tn * 4 + side_vmem
    out, *cast = pl.pallas_call(
        functools.partial(_mm_plain_kernel, side_t=tuple(s[2] for s in side)),
        grid=grid,
        in_specs=[pl.BlockSpec((tm, k), lambda j, i: (i, 0)),
                  pl.BlockSpec((k, tn), lambda j, i: (0, j + col_block0))] + side_in,
        out_specs=[pl.BlockSpec((tm, tn), lambda j, i: (i, j))] + side_out,
        out_shape=[jax.ShapeDtypeStruct((m, n_out), out_dtype)] + side_shapes,
        compiler_params=_cparams(2, vmem),
        name=name,
    )(a, w, *[s[0] for s in side])
    return out, cast


def _mm_resid_kernel(*refs, n_parts, side_t):
    n_side = len(side_t)
    a_refs = refs[:n_parts]
    w_refs = refs[n_parts:2 * n_parts]
    x_ref, gate_ref, lgate_ref = refs[2 * n_parts:2 * n_parts + 3]
    side_in = refs[2 * n_parts + 3:2 * n_parts + 3 + n_side]
    o_ref = refs[2 * n_parts + 3 + n_side]
    side_out = refs[2 * n_parts + 4 + n_side:]
    _side_cast(side_in, side_out, side_t)
    y = _dot(a_refs[0][...], w_refs[0][...])
    for a_ref, w_ref in zip(a_refs[1:], w_refs[1:]):
        y = y + _dot(a_ref[...], w_ref[...])
    o_ref[...] = x_ref[...] + (gate_ref[0] + lgate_ref[0]) * y


def _matmul_residual(a_parts, w, x, mod_rows, layer_rows, layer, gate_idx, seq, tm, tn, name, side=(),
                     w_buffers=2):
    n_parts = len(a_parts)
    m, kp = a_parts[0].shape
    n = w.shape[1]
    per_seq = seq // tm
    grid = (n // tn, m // tm)
    side_in, side_out, side_shapes, side_vmem = _side_cast_plan(side, grid[1], grid[0] * grid[1])
    vmem = n_parts * (w_buffers * kp * tn * 2 + 2 * tm * kp * 2) + 5 * tm * tn * 4 + side_vmem
    a_specs = [pl.BlockSpec((tm, kp), lambda j, i: (i, 0)) for _ in range(n_parts)]
    w_specs = [pl.BlockSpec((kp, tn), functools.partial(lambda j, i, p: (p, j), p=p),
                            pipeline_mode=pl.Buffered(w_buffers))
               for p in range(n_parts)]
    out, *cast = pl.pallas_call(
        functools.partial(_mm_resid_kernel, n_parts=n_parts, side_t=tuple(s[2] for s in side)),
        grid=grid,
        in_specs=a_specs + w_specs + [
            pl.BlockSpec((tm, tn), lambda j, i: (i, j)),
            pl.BlockSpec((1, 1, tn), lambda j, i: ((i // per_seq) * N_MOD + gate_idx, 0, j)),
            pl.BlockSpec((1, 1, tn), lambda j, i: (layer * N_MOD + gate_idx, 0, j))] + side_in,
        out_specs=[pl.BlockSpec((tm, tn), lambda j, i: (i, j))] + side_out,
        out_shape=[jax.ShapeDtypeStruct((m, n), F32)] + side_shapes,
        compiler_params=_cparams(2, vmem),
        name=name,
    )(*a_parts, *([w] * n_parts), x, mod_rows, layer_rows, *[s[0] for s in side])
    return out, cast


HALO = 8
FFN_SUB_ROWS = 256
GDN_SUB_ROWS = 128


def _conv_begin(ybuf_ref, tm, tiles_per_seq):
    starts_sequence = (pl.program_id(1) % tiles_per_seq) == 0

    @pl.when(starts_sequence)
    def _():
        ybuf_ref[0:HALO, :] = jnp.zeros((HALO, ybuf_ref.shape[1]), F32)

    @pl.when(jnp.logical_not(starts_sequence))
    def _():
        ybuf_ref[0:HALO, :] = ybuf_ref[tm:tm + HALO, :]


def _conv_rows(y, ybuf_ref, cw_ref, r0):
    rows = y.shape[0]
    taps = cw_ref.shape[0]
    ybuf_ref[HALO + r0:HALO + r0 + rows, :] = y
    out = cw_ref[taps - 1:taps, :] * y
    for j in range(taps - 1):
        off = HALO + r0 - (taps - 1) + j
        out = out + cw_ref[j:j + 1, :] * ybuf_ref[off:off + rows, :]
    return out


def _mm_ffn_up_kernel(*refs, tiles_per_seq, sub, side_t):
    n_side = len(side_t)
    a_ref, wg_ref, wv_ref, cw_ref, cb_ref = refs[:5]
    side_in = refs[5:5 + n_side]
    o_ref = refs[5 + n_side]
    side_out = refs[6 + n_side:6 + 2 * n_side]
    ybuf_ref = refs[6 + 2 * n_side]
    _side_cast(side_in, side_out, side_t)
    tm = a_ref.shape[0]
    _conv_begin(ybuf_ref, tm, tiles_per_seq)
    for r0 in range(0, tm, sub):
        a = a_ref[r0:r0 + sub, :]
        gate = _conv_rows(_dot(a, wg_ref[...]), ybuf_ref, cw_ref, r0) + cb_ref[...]
        o_ref[r0:r0 + sub, :] = (_gelu(gate) * _dot(a, wv_ref[...])).astype(o_ref.dtype)


def _ffn_up(h, w_up, conv_w, conv_b, seq, tm, tn, side=()):
    m, k = h.shape
    d_ff = w_up.shape[1] // 2
    nb = d_ff // tn
    sub = _tile(tm, FFN_SUB_ROWS, 8)
    grid = (nb, m // tm)
    side_in, side_out, side_shapes, side_vmem = _side_cast_plan(side, grid[1], grid[0] * grid[1])
    vmem = 2 * 2 * k * tn * 2 + 2 * tm * k * 2 + 3 * tm * tn * 4 + 8 * sub * tn * 4 + side_vmem
    out, *cast = pl.pallas_call(
        functools.partial(_mm_ffn_up_kernel, tiles_per_seq=seq // tm, sub=sub,
                          side_t=tuple(s[2] for s in side)),
        grid=grid,
        in_specs=[pl.BlockSpec((tm, k), lambda j, i: (i, 0)),
                  pl.BlockSpec((k, tn), lambda j, i: (0, j)),
                  pl.BlockSpec((k, tn), lambda j, i: (0, j + nb)),
                  pl.BlockSpec((conv_w.shape[0], tn), lambda j, i: (0, j)),
                  pl.BlockSpec((1, tn), lambda j, i: (0, j))] + side_in,
        out_specs=[pl.BlockSpec((tm, tn), lambda j, i: (i, j))] + side_out,
        out_shape=[jax.ShapeDtypeStruct((m, d_ff), BF16)] + side_shapes,
        scratch_shapes=[pltpu.VMEM((tm + HALO, tn), F32)],
        compiler_params=_cparams(2, vmem),
        name="ffn_up",
    )(h, w_up, w_up, conv_w, conv_b.reshape(1, d_ff), *[s[0] for s in side])
    return out, cast


def _mm_gdn_proj_kernel(*refs, kinds, tiles_per_seq, sub):
    n = len(kinds)
    n_conv = sum(kind[0] != "plain" for kind in kinds)
    a_ref = refs[0]
    w_refs = refs[1:1 + n]
    cw_refs = iter(refs[1 + n:1 + n + n_conv])
    o_refs = refs[1 + n + n_conv:1 + 2 * n + n_conv]
    ybuf_refs = iter(refs[1 + 2 * n + n_conv:])
    conv_refs = [None if kind[0] == "plain" else (next(cw_refs), next(ybuf_refs)) for kind in kinds]
    tm = a_ref.shape[0]
    for conv in conv_refs:
        if conv is not None:
            _conv_begin(conv[1], tm, tiles_per_seq)
    for r0 in range(0, tm, sub):
        a = a_ref[r0:r0 + sub, :]
        for kind, w_ref, conv, o_ref in zip(kinds, w_refs, conv_refs, o_refs):
            y = _dot(a, w_ref[...])
            if conv is not None:
                y = _silu(_conv_rows(y, conv[1], conv[0], r0))
            if kind[0] != "l2":
                o_ref[r0:r0 + sub, :] = y.astype(o_ref.dtype)
                continue
            for g in range(y.shape[1] // HEAD_DIM):
                cols = slice(g * HEAD_DIM, (g + 1) * HEAD_DIM)
                yg = y[:, cols]
                inv = lax.rsqrt(jnp.sum(yg * yg, axis=-1, keepdims=True) + EPS)
                o_ref[r0:r0 + sub, cols] = (yg * (inv * kind[1])).astype(o_ref.dtype)


def _gdn_proj(h, w_in, conv_w, streams, n_out, seq, tm, tn, name):
    m, k = h.shape
    sub = _tile(tm, GDN_SUB_ROWS, 8)
    n = len(streams)
    conv_streams = [s for s in streams if s[1][0] != "plain"]
    vmem = n * (2 * k * tn * 2 + 3 * tm * tn * 4 + 8 * sub * tn * 4) + 2 * tm * k * 2
    w_specs = [pl.BlockSpec((k, tn), functools.partial(lambda j, i, c0: (0, j + c0), c0=s[0]))
               for s in streams]
    cw_specs = [pl.BlockSpec((conv_w.shape[0], tn), functools.partial(lambda j, i, c0: (0, j + c0), c0=s[0]))
                for s in conv_streams]
    out_spec = pl.BlockSpec((tm, tn), lambda j, i: (i, j))
    return pl.pallas_call(
        functools.partial(_mm_gdn_proj_kernel, kinds=tuple(s[1] for s in streams),
                          tiles_per_seq=seq // tm, sub=sub),
        grid=(n_out // tn, m // tm),
        in_specs=[pl.BlockSpec((tm, k), lambda j, i: (i, 0))] + w_specs + cw_specs,
        out_specs=[out_spec] * n,
        out_shape=[jax.ShapeDtypeStruct((m, n_out), s[2]) for s in streams],
        scratch_shapes=[pltpu.VMEM((tm + HALO, tn), F32) for _ in conv_streams],
        compiler_params=_cparams(2, vmem),
        name=name,
    )(h, *([w_in] * n), *([conv_w] * len(conv_streams)))


def _chunk_masks(blk, ch):
    shift = ch.bit_length() - 1
    r = lax.broadcasted_iota(jnp.int32, (blk, blk), 0)
    c = lax.broadcasted_iota(jnp.int32, (blk, blk), 1)
    same_chunk = jnp.right_shift(r, shift) == jnp.right_shift(c, shift)
    return (jnp.logical_and(same_chunk, r >= c), jnp.logical_and(same_chunk, r > c), same_chunk)


def _mm_gates_kernel(a_ref, w_ref, alog_ref, dt_ref, o_ref, *, n_heads):
    tm = a_ref.shape[0]
    blk = GDN_BLOCK
    lower_incl, _, same_chunk = _chunk_masks(blk, GDN_CHUNK)
    cum_mat = lower_incl.astype(BF16)
    tot_mat = same_chunk.astype(BF16)
    lane = lax.broadcasted_iota(jnp.int32, (blk, HEAD_DIM), 1)
    w_ab = w_ref[...]
    w = jnp.concatenate([w_ab, w_ab[:n_heads],
                         jnp.zeros((HEAD_DIM - 3 * n_heads, w_ab.shape[1]), F32)], axis=0).astype(BF16)
    for r0 in range(0, tm, blk):
        y = _dot_nt(a_ref[r0:r0 + blk, :], w)
        g = -jnp.exp(alog_ref[...]) * _softplus(y + dt_ref[...])
        hi, mid, lo = _split3(g)
        cum = _dot(cum_mat, hi) + _dot(cum_mat, mid) + _dot(cum_mat, lo)
        tot = _dot(tot_mat, hi) + _dot(tot_mat, mid) + _dot(tot_mat, lo)
        o_ref[r0:r0 + blk, :] = jnp.where(lane < n_heads, cum,
                                          jnp.where(lane < 2 * n_heads, _sigmoid(y), tot))


def _gdn_gates(h, w_in_t, layer, a_log, dt_bias, tm):
    m, k = h.shape
    n_heads = a_log.shape[0]
    tn = HEAD_DIM
    n_rows = w_in_t.shape[1]
    assert 3 * n_heads <= tn and tm % GDN_BLOCK == 0 and n_rows % (2 * n_heads) == 0
    zeros_h = jnp.zeros((n_heads,), F32)
    pad = lambda v: jnp.concatenate([v, zeros_h, v, jnp.zeros((tn - 3 * n_heads,), F32)]).reshape(1, tn)
    vmem = 4 * tn * k * 4 + 2 * tm * k * 2 + 4 * tm * tn * 4
    return pl.pallas_call(
        functools.partial(_mm_gates_kernel, n_heads=n_heads),
        grid=(m // tm,),
        in_specs=[pl.BlockSpec((tm, k), lambda i: (i, 0)),
                  pl.BlockSpec((None, 2 * n_heads, k), lambda i: (layer, n_rows // (2 * n_heads) - 1, 0)),
                  pl.BlockSpec((1, tn), lambda i: (0, 0)),
                  pl.BlockSpec((1, tn), lambda i: (0, 0))],
        out_specs=pl.BlockSpec((tm, tn), lambda i: (i, 0)),
        out_shape=jax.ShapeDtypeStruct((m, tn), F32),
        compiler_params=_cparams(1, vmem),
        name="gdn_gates",
    )(h, w_in_t, pad(a_log), pad(dt_bias))


def _sb_kernel(q_ref, k_ref, v_ref, o_ref, *, tq, n_pack, scale):
    qi = pl.program_id(2)
    row = lax.broadcasted_iota(jnp.int32, (tq, tq), 0)
    col = lax.broadcasted_iota(jnp.int32, (tq, tq), 1)
    later = (row > col).astype(BF16)
    below_diag = col < row
    head_cols = [slice(p * HEAD_DIM, (p + 1) * HEAD_DIM) for p in range(n_pack)]
    qs = [q_ref[:, cols] for cols in head_cols]

    def key_tile(kb, cs, accs, on_diagonal):
        start = pl.multiple_of(kb * tq, tq)
        keep = (lambda x: jnp.where(below_diag, x, 0.0)) if on_diagonal else (lambda x: x)
        heads = range(n_pack)
        z = [_dot_nt(qs[p], k_ref[pl.ds(start, tq), head_cols[p]]) * scale for p in heads]
        sp = [jnp.maximum(z[p], 0.0) + jnp.log(1.0 + jnp.exp(-jnp.abs(z[p]))) for p in heads]
        log_beta = [z[p] - sp[p] for p in heads]
        log_1m = [keep(-sp[p]) for p in heads]
        tail = [_dot(log_1m[p].astype(BF16), later) + cs[p] for p in heads]
        w = [keep(jnp.exp(log_beta[p] + tail[p])).astype(BF16) for p in heads]
        new_accs = [accs[p] + _dot(w[p], v_ref[pl.ds(start, tq), head_cols[p]]) for p in heads]
        new_cs = [cs[p] + jnp.sum(log_1m[p], axis=1, keepdims=True) for p in heads]
        c_max = functools.reduce(jnp.maximum, [jnp.max(c) for c in new_cs])
        return kb - 1, tuple(new_cs), tuple(new_accs), c_max > -F32_EXP_UNDERFLOW

    def cond(carry):
        kb, _, _, live = carry
        return jnp.logical_and(kb >= 0, live)

    zeros = lambda width: tuple(jnp.zeros((tq, width), F32) for _ in range(n_pack))
    carry = key_tile(qi, zeros(1), zeros(HEAD_DIM), True)
    _, _, accs, _ = lax.while_loop(cond, lambda c: key_tile(c[0], c[1], c[2], False), carry)
    for cols, acc in zip(head_cols, accs):
        o_ref[:, cols] = acc.astype(o_ref.dtype)


def _stick_breaking(proj, bsz, seq, n_heads, tq):
    t = proj.shape[0]
    nq = seq // tq
    n_pack = SB_HEAD_PACK if n_heads % SB_HEAD_PACK == 0 else 1
    groups = n_heads // n_pack
    width = n_pack * HEAD_DIM
    vmem = 2 * 2 * seq * width * 2 + 4 * tq * width * 2 + n_pack * 12 * tq * tq * 4
    return pl.pallas_call(
        functools.partial(_sb_kernel, tq=tq, n_pack=n_pack, scale=HEAD_DIM ** -0.5),
        grid=(bsz, groups, nq),
        in_specs=[pl.BlockSpec((tq, width), lambda b, h, i: (b * nq + i, h)),
                  pl.BlockSpec((seq, width), lambda b, h, i: (b, groups + h)),
                  pl.BlockSpec((seq, width), lambda b, h, i: (b, 2 * groups + h))],
        out_specs=pl.BlockSpec((tq, width), lambda b, h, i: (b * nq + i, h)),
        out_shape=jax.ShapeDtypeStruct((t, n_heads * HEAD_DIM), BF16),
        compiler_params=_cparams(3, vmem),
        name="stick_breaking",
    )(proj, proj, proj)


def _sgu_kernel(u_ref, v_ref, gain_ref, w_ref, b_ref, o_ref):
    rows = u_ref.shape[0]
    r = lax.broadcasted_iota(jnp.int32, (SGU_LEN, SGU_LEN), 0)
    c = lax.broadcasted_iota(jnp.int32, (SGU_LEN, SGU_LEN), 1)
    w = jnp.where(r >= c, w_ref[0], 0.0).astype(BF16)
    bias = b_ref[0]
    gain = gain_ref[0]
    for n in range(rows // SGU_LEN):
        sl = slice(n * SGU_LEN, (n + 1) * SGU_LEN)
        u = _gelu(u_ref[sl, :].astype(F32))
        v = _gelu(v_ref[sl, :].astype(F32))
        v = v * lax.rsqrt(jnp.mean(v * v, axis=-1, keepdims=True) + EPS) * gain
        mixed = _dot(w, v.astype(BF16)) + bias
        o_ref[sl, :] = (u * mixed).astype(o_ref.dtype)


def _spatial_gating(proj, sgu_gain, sgu_w, sgu_b, n_groups, u_block0, rows):
    t = proj.shape[0]
    return pl.pallas_call(
        _sgu_kernel,
        grid=(n_groups, t // rows),
        in_specs=[pl.BlockSpec((rows, HEAD_DIM), lambda g, i: (i, u_block0 + g)),
                  pl.BlockSpec((rows, HEAD_DIM), lambda g, i: (i, u_block0 + n_groups + g)),
                  pl.BlockSpec((1, 1, HEAD_DIM), lambda g, i: (g, 0, 0)),
                  pl.BlockSpec((1, SGU_LEN, SGU_LEN), lambda g, i: (g, 0, 0)),
                  pl.BlockSpec((1, SGU_LEN, 1), lambda g, i: (g, 0, 0))],
        out_specs=pl.BlockSpec((rows, HEAD_DIM), lambda g, i: (i, g)),
        out_shape=jax.ShapeDtypeStruct((t, n_groups * HEAD_DIM), BF16),
        compiler_params=_cparams(2, 16 * rows * HEAD_DIM * 4),
        name="spatial_gating",
    )(proj, proj, sgu_gain.reshape(n_groups, 1, HEAD_DIM), sgu_w,
      sgu_b.reshape(n_groups, SGU_LEN, 1))


def _neumann_solve(p_mats, xs, n_factors):
    idx = range(len(xs))
    n = p_mats[0].shape[0]
    eye = (lax.broadcasted_iota(jnp.int32, (n, n), 0)
           == lax.broadcasted_iota(jnp.int32, (n, n), 1)).astype(F32)
    ts = [eye + p_mats[i] for i in idx]
    for _ in range(1, n_factors):
        pb = [p_mats[i].astype(BF16) for i in idx]
        p_mats = [_dot(pb[i], pb[i]) for i in idx]
        ts = [ts[i] + _dot(p_mats[i].astype(BF16), ts[i].astype(BF16)) for i in idx]
    return [_dot(ts[i].astype(BF16), xs[i].astype(BF16)) for i in idx]


def _gdn_kernel(q_ref, k_ref, v_ref, z_ref, gb_ref, gain_ref, o_ref, state_ref, *, n_heads, n_pack):
    hg = pl.program_id(1)
    n_rows = q_ref.shape[0]
    blk, ch = GDN_BLOCK, GDN_CHUNK
    n_ch = blk // ch
    n_factors = ch.bit_length() - 1

    @pl.when(pl.program_id(2) == 0)
    def _():
        state_ref[...] = jnp.zeros(state_ref.shape, F32)

    lower_incl, lower_strict, _ = _chunk_masks(blk, ch)
    lane = lax.broadcasted_iota(jnp.int32, (blk, HEAD_DIM), 1)
    gain = gain_ref[...]

    def pick(gb, lane_idx):
        return jnp.sum(jnp.where(lane == lane_idx, gb, 0.0), axis=1, keepdims=True)

    def diff_operands(gc):
        chi, cmid, clo = (part.astype(F32) for part in _split3(jnp.broadcast_to(gc, (blk, HEAD_DIM))))
        left = jnp.where(lane == 0, chi, jnp.where(lane == 1, cmid, jnp.where(
            lane == 2, clo, jnp.where(lane < 6, 1.0, 0.0))))
        right = jnp.where(lane == 3, -chi, jnp.where(lane == 4, -cmid, jnp.where(
            lane == 5, -clo, jnp.where(lane < 3, 1.0, 0.0))))
        return left.astype(BF16), right.astype(BF16)

    head_cols = [slice(p * HEAD_DIM, (p + 1) * HEAD_DIM) for p in range(n_pack)]
    heads = range(n_pack)

    def block(b, states):
        rows = pl.ds(pl.multiple_of(b * blk, blk), blk)
        gb = gb_ref[rows, :]
        q = [q_ref[rows, cols].astype(F32) for cols in head_cols]
        k = [k_ref[rows, cols].astype(F32) for cols in head_cols]
        v = [v_ref[rows, cols].astype(F32) for cols in head_cols]
        gc = [pick(gb, hg * n_pack + p) for p in heads]
        beta = [pick(gb, hg * n_pack + p + n_heads) for p in heads]
        gl = [pick(gb, hg * n_pack + p + 2 * n_heads) for p in heads]
        diff = [_dot_nt(*diff_operands(g)) for g in gc]
        decay = [jnp.exp(jnp.where(lower_incl, d, -1e30)) for d in diff]
        eg = [jnp.exp(g) for g in gc]
        kb = [k[p] * beta[p] for p in heads]
        kbf = [k[p].astype(BF16) for p in heads]
        p_mat = [jnp.where(lower_strict, _dot_nt((-kb[p]).astype(BF16), kbf[p]) * decay[p], 0.0)
                 for p in heads]
        qk = [(_dot_nt(q[p].astype(BF16), kbf[p]) * decay[p]).astype(BF16) for p in heads]
        x = [jnp.concatenate([v[p] * beta[p], kb[p] * eg[p]], axis=1) for p in heads]
        x = _neumann_solve(p_mat, x, n_factors)
        u = [x[p][:, :HEAD_DIM] for p in heads]
        w = [x[p][:, HEAD_DIM:].astype(BF16) for p in heads]
        q_dec = [(q[p] * eg[p]).astype(BF16) for p in heads]
        k_dec = [(k[p] * jnp.exp(gl[p] - gc[p])).astype(BF16) for p in heads]
        g_tot = [jnp.exp(gl[p]) for p in heads]

        states = list(states)
        o_inter = [[] for _ in heads]
        v_new = [[] for _ in heads]
        for ci in range(n_ch):
            sl = slice(ci * ch, (ci + 1) * ch)
            sb = [states[p].astype(BF16) for p in heads]
            v_c = [u[p][sl] - _dot(w[p][sl], sb[p]) for p in heads]
            for p in heads:
                o_inter[p].append(_dot(q_dec[p][sl], sb[p]))
                v_new[p].append(v_c[p])
            states = [states[p] * g_tot[p][ci * ch:ci * ch + 1, :]
                      + _dot_tn(k_dec[p][sl], v_c[p].astype(BF16)) for p in heads]
        o = [jnp.concatenate(o_inter[p], axis=0)
             + _dot(qk[p], jnp.concatenate(v_new[p], axis=0).astype(BF16)) for p in heads]
        for p, cols in enumerate(head_cols):
            on = o[p] * lax.rsqrt(jnp.mean(o[p] * o[p], axis=-1, keepdims=True) + EPS) * gain
            o_ref[rows, cols] = (on * _silu(z_ref[rows, cols].astype(F32))).astype(o_ref.dtype)
        return tuple(states)

    states = lax.fori_loop(0, n_rows // blk, block, tuple(state_ref[p] for p in range(n_pack)),
                           unroll=4)
    for p in range(n_pack):
        state_ref[p] = states[p]


def _gated_delta(q, k, v, z, gates, o_gain, bsz, seq, n_heads):
    t = q.shape[0]
    n_pack = GDN_HEAD_PACK if n_heads % GDN_HEAD_PACK == 0 else 1
    width = n_pack * HEAD_DIM
    rows = _tile(seq, 2048, GDN_BLOCK)
    tiles = seq // rows
    head_spec = pl.BlockSpec((rows, width), lambda b, h, r: (b * tiles + r, h))
    io_bytes = sum(jnp.dtype(a.dtype).itemsize for a in (q, k, v, z)) + 2
    vmem = 2 * (rows * width * io_bytes + rows * HEAD_DIM * 4) + n_pack * 40 * GDN_BLOCK * GDN_BLOCK * 4
    return pl.pallas_call(
        functools.partial(_gdn_kernel, n_heads=n_heads, n_pack=n_pack),
        grid=(bsz, n_heads // n_pack, tiles),
        in_specs=[head_spec, head_spec, head_spec, head_spec,
                  pl.BlockSpec((rows, HEAD_DIM), lambda b, h, r: (b * tiles + r, 0)),
                  pl.BlockSpec((1, HEAD_DIM), lambda b, h, r: (0, 0))],
        out_specs=head_spec,
        out_shape=jax.ShapeDtypeStruct((t, n_heads * HEAD_DIM), BF16),
        scratch_shapes=[pltpu.VMEM((n_pack, HEAD_DIM, HEAD_DIM), F32)],
        compiler_params=_cparams(3, vmem),
        name="gated_delta",
    )(q, k, v, z, gates, o_gain.reshape(1, HEAD_DIM))


def kernel(x, c, ada_w, ada_b, ada_layer, norm_mix, norm_ffn, norm_final, ev_w_in, ev_w_out, sgu_gain, sgu_w, sgu_b, gdn_w_in, gdn_conv, gdn_a_log, gdn_dt_bias, gdn_o_gain, gdn_w_out, ffn_w_up, ffn_conv, ffn_conv_b, ffn_w_down):
    bsz, seq, d = x.shape
    depth = ada_layer.shape[0]
    t = bsz * seq
    d_ff = ffn_w_down.shape[1]
    n_sb = d // 2 // HEAD_DIM
    n_sgu = sgu_gain.shape[1]
    n_gdn = gdn_a_log.shape[1]
    assert sgu_gain.shape[2] == HEAD_DIM and d == n_gdn * HEAD_DIM and sgu_w.shape[2] == SGU_LEN
    assert n_sgu == n_sb and seq % GDN_BLOCK == 0
    tm = _tile(seq, 1024)

    gdn_w_in_t = jnp.swapaxes(gdn_w_in, 1, 2)

    def mixer_weights(layer):
        if layer % 2 == 0:
            return (ev_w_in, layer // 2, False), (ev_w_out, layer // 2, False)
        return (gdn_w_in_t, layer // 2, True), (gdn_w_out, layer // 2, False)

    assert depth > 0
    w_mix_in, w_mix_out, w_up = _to_bf16(ev_w_in, 0), _to_bf16(ev_w_out, 0), _to_bf16(ffn_w_up, 0)

    mod_rows = _ada_project(c, ada_w, ada_b).reshape(bsz * N_MOD, 1, d)
    layer_rows = ada_layer.reshape(depth * N_MOD, 1, d)
    x = x.reshape(t, d)

    for layer in range(depth):
        h = _norm_modulate(x, norm_mix[layer], mod_rows, layer_rows, layer, 0, seq)
        if layer % 2 == 0:
            e = layer // 2
            n_proj = 5 * (d // 2)
            proj, early_cast = _matmul(h, w_mix_in, 0, n_proj, BF16, tm, _tile(n_proj, 1024),
                                       "even_in_proj",
                                       side=[mixer_weights(layer + 1)[0]] if layer + 1 < depth else [])
            o_a = _stick_breaking(proj, bsz, seq, n_sb, _tile(seq, 256))
            o_b = _spatial_gating(proj, sgu_gain[e], sgu_w[e], sgu_b[e], n_sgu, 3 * n_sb,
                                  _tile(seq, 2048))
            x, _ = _matmul_residual([o_a, o_b], w_mix_out, x, mod_rows, layer_rows, layer, 2, seq,
                                    tm, _tile(d, 1024), "even_out_proj", w_buffers=1)
        else:
            o = layer // 2
            tn = _tile(d, 512)
            nb = d // tn
            conv_w = gdn_conv[o]
            q, k = _gdn_proj(h, w_mix_in, conv_w,
                             [(0, ("l2", HEAD_DIM ** -0.5), BF16), (nb, ("l2", 1.0), BF16)],
                             d, seq, tm, tn, "gdn_qk_proj")
            v, z = _gdn_proj(h, w_mix_in, conv_w,
                             [(2 * nb, ("silu",), F32), (3 * nb, ("plain",), F32)],
                             d, seq, tm, tn, "gdn_vz_proj")
            gates = _gdn_gates(h, gdn_w_in_t, o, gdn_a_log[o], gdn_dt_bias[o], tm)
            y = _gated_delta(q, k, v, z, gates, gdn_o_gain[o], bsz, seq, n_gdn)
            x, _ = _matmul_residual([y], w_mix_out, x, mod_rows, layer_rows, layer, 2, seq,
                                    tm, _tile(d, 1024), "gdn_out_proj", w_buffers=1)
        has_next = layer + 1 < depth
        side = [(ffn_w_down, layer, False)]
        if has_next:
            next_in, next_out = mixer_weights(layer + 1)
            side += ([] if layer % 2 == 0 else [next_in]) + [next_out, (ffn_w_up, layer + 1, False)]
        h = _norm_modulate(x, norm_ffn[layer], mod_rows, layer_rows, layer, 3, seq)
        f, cast = _ffn_up(h, w_up, ffn_conv[layer], ffn_conv_b[layer], seq, tm, _tile(d_ff, 512),
                          side=side)
        w_down = cast[0]
        if has_next:
            w_mix_in = early_cast[0] if layer % 2 == 0 else cast[1]
            w_mix_out, w_up = cast[-2:]
        x, _ = _matmul_residual([f], w_down, x, mod_rows, layer_rows, layer, 5, seq,
                                tm, _tile(d, 512), "ffn_down_proj", w_buffers=1)
    return _final_norm(x, norm_final).reshape(bsz, seq, d)
```

```python
import functools
import math

import jax
import jax.numpy as jnp
from jax import lax
from jax.experimental import pallas as pl
from jax.experimental.pallas import tpu as pltpu

F32 = jnp.float32
BF16 = jnp.bfloat16

HEAD_DIM = 128
SGU_LEN = 128
GDN_CHUNK = 64
GDN_BLOCK = 256
SB_HEAD_PACK = 4
GDN_HEAD_PACK = 4
N_MOD = 6
EPS = 1e-6
F32_EXP_UNDERFLOW = 104.0
V7X_VMEM_BYTES = 64 * 1024 * 1024
VMEM_CAP = V7X_VMEM_BYTES - 6 * 1024 * 1024
MIB = 1024 * 1024


def _cparams(n_axes, vmem_bytes):
    limit = int(min(VMEM_CAP, max(32 * MIB, vmem_bytes * 5 // 4 + 4 * MIB)))
    return pltpu.CompilerParams(dimension_semantics=("arbitrary",) * n_axes,
                                vmem_limit_bytes=limit)


def _tile(n, pref, unit=128):
    if n <= pref:
        return n
    t = pref // unit * unit
    while n % t:
        t -= unit
    assert t > 0, (n, pref)
    return t


def _dot(a, b):
    return jnp.dot(a, b, preferred_element_type=F32)


def _dot_nt(a, b):
    return lax.dot_general(a, b, (((1,), (1,)), ((), ())), preferred_element_type=F32)


def _dot_tn(a, b):
    return lax.dot_general(a, b, (((0,), (0,)), ((), ())), preferred_element_type=F32)


def _split3(x):
    hi = x.astype(BF16)
    r = x - hi.astype(F32)
    mid = r.astype(BF16)
    lo = (r - mid.astype(F32)).astype(BF16)
    return hi, mid, lo


def _softplus(x):
    return jnp.maximum(x, 0.0) + jnp.log1p(jnp.exp(-jnp.abs(x)))


def _sigmoid(x):
    return 1.0 / (1.0 + jnp.exp(-x))


def _silu(x):
    return x * _sigmoid(x)


def _gelu(x):
    c = math.sqrt(2.0 / math.pi)
    return x * (0.5 * (1.0 + jnp.tanh(c * (x + 0.044715 * (x * x * x)))))


def _cast_kernel(w_ref, o_ref):
    o_ref[...] = w_ref[...].astype(o_ref.dtype)


def _to_bf16(w, layer):
    _, rows, cols = w.shape
    tr = _tile(rows, max(16, 4 * MIB // (cols * 4)), unit=16)
    return pl.pallas_call(
        _cast_kernel,
        grid=(rows // tr,),
        in_specs=[pl.BlockSpec((None, tr, cols), lambda i: (layer, i, 0))],
        out_specs=pl.BlockSpec((tr, cols), lambda i: (i, 0)),
        out_shape=jax.ShapeDtypeStruct((rows, cols), BF16),
        compiler_params=_cparams(1, 2 * tr * cols * 6),
        name="weights_to_bf16",
    )(w)


def _side_cast_plan(side, n_inner, n_steps):
    in_specs, out_specs, out_shapes, vmem = [], [], [], 0
    for w, layer, transposed in side:
        step = lambda j, i, last: jnp.minimum(j * n_inner + i, last)
        if transposed:
            _, cols, rows = w.shape
            tc = HEAD_DIM * pl.cdiv(pl.cdiv(cols, HEAD_DIM), n_steps)
            last = pl.cdiv(cols, tc) - 1
            in_specs.append(pl.BlockSpec((None, tc, rows), functools.partial(
                lambda j, i, layer, last: (layer, step(j, i, last), 0), layer=layer, last=last)))
            out_specs.append(pl.BlockSpec((rows, tc), functools.partial(
                lambda j, i, last: (0, step(j, i, last)), last=last)))
            vmem += 2 * tc * rows * 6 + tc * rows * 4
        else:
            _, rows, cols = w.shape
            tr = 16
            while rows % tr or rows // tr > n_steps:
                tr += 16
            last = rows // tr - 1
            in_specs.append(pl.BlockSpec((None, tr, cols), functools.partial(
                lambda j, i, layer, last: (layer, step(j, i, last), 0), layer=layer, last=last)))
            out_specs.append(pl.BlockSpec((tr, cols), functools.partial(
                lambda j, i, last: (step(j, i, last), 0), last=last)))
            vmem += 2 * tr * cols * 6
        out_shapes.append(jax.ShapeDtypeStruct((rows, cols), BF16))
    return in_specs, out_specs, out_shapes, vmem


def _side_cast(in_refs, out_refs, transposed):
    for w_ref, o_ref, flip in zip(in_refs, out_refs, transposed):
        w = w_ref[...]
        o_ref[...] = (w.T if flip else w).astype(o_ref.dtype)


def _ada_kernel(c_ref, w_ref, b_ref, o_ref):
    c = c_ref[...]
    o_ref[...] = _dot(_silu(c).astype(BF16), w_ref[...].astype(BF16)) + b_ref[...]


def _ada_project(c, ada_w, ada_b):
    bsz, d = c.shape
    n = ada_w.shape[1]
    rows = 8
    tn = _tile(n, 512)
    c_pad = jnp.zeros((rows, d), F32).at[:bsz].set(c)
    out = pl.pallas_call(
        _ada_kernel,
        grid=(n // tn,),
        in_specs=[pl.BlockSpec((rows, d), lambda j: (0, 0)),
                  pl.BlockSpec((d, tn), lambda j: (0, j)),
                  pl.BlockSpec((1, tn), lambda j: (0, j))],
        out_specs=pl.BlockSpec((rows, tn), lambda j: (0, j)),
        out_shape=jax.ShapeDtypeStruct((rows, n), F32),
        compiler_params=_cparams(1, 2 * d * tn * 4 + d * tn * 2),
        name="ada_project",
    )(c_pad, ada_w, ada_b.reshape(1, n))
    return out[:bsz]


def _norm_mod_kernel(x_ref, g_ref, sh_ref, sc_ref, lsh_ref, lsc_ref, o_ref):
    x = x_ref[...]
    y = x * lax.rsqrt(jnp.mean(x * x, axis=-1, keepdims=True) + EPS) * g_ref[...]
    scale = 1.0 + (sc_ref[0] + lsc_ref[0])
    shift = sh_ref[0] + lsh_ref[0]
    o_ref[...] = (y * scale + shift).astype(o_ref.dtype)


def _norm_modulate(x, gain, mod_rows, layer_rows, layer, shift_idx, seq):
    t, d = x.shape
    tr = _tile(seq, 512)
    per_seq = seq // tr
    mod_spec = lambda idx: pl.BlockSpec((1, 1, d), lambda i: ((i // per_seq) * N_MOD + idx, 0, 0))
    lay_spec = lambda idx: pl.BlockSpec((1, 1, d), lambda i: (layer * N_MOD + idx, 0, 0))
    return pl.pallas_call(
        _norm_mod_kernel,
        grid=(t // tr,),
        in_specs=[pl.BlockSpec((tr, d), lambda i: (i, 0)),
                  pl.BlockSpec((1, d), lambda i: (0, 0)),
                  mod_spec(shift_idx), mod_spec(shift_idx + 1),
                  lay_spec(shift_idx), lay_spec(shift_idx + 1)],
        out_specs=pl.BlockSpec((tr, d), lambda i: (i, 0)),
        out_shape=jax.ShapeDtypeStruct((t, d), BF16),
        compiler_params=_cparams(1, 2 * tr * d * 6),
        name="norm_modulate",
    )(x, gain.reshape(1, d), mod_rows, mod_rows, layer_rows, layer_rows)


def _final_norm_kernel(x_ref, g_ref, o_ref):
    x = x_ref[...]
    o_ref[...] = x * lax.rsqrt(jnp.mean(x * x, axis=-1, keepdims=True) + EPS) * g_ref[...]


def _final_norm(x, gain):
    t, d = x.shape
    tr = _tile(t, 256)
    return pl.pallas_call(
        _final_norm_kernel,
        grid=(t // tr,),
        in_specs=[pl.BlockSpec((tr, d), lambda i: (i, 0)),
                  pl.BlockSpec((1, d), lambda i: (0, 0))],
        out_specs=pl.BlockSpec((tr, d), lambda i: (i, 0)),
        out_shape=jax.ShapeDtypeStruct((t, d), F32),
        compiler_params=_cparams(1, 2 * tr * d * 8),
        name="final_norm",
    )(x, gain.reshape(1, d))


def _mm_plain_kernel(*refs, side_t):
    n_side = len(side_t)
    a_ref, w_ref = refs[:2]
    o_ref = refs[2 + n_side]
    _side_cast(refs[2:2 + n_side], refs[3 + n_side:], side_t)
    o_ref[...] = _dot(a_ref[...], w_ref[...]).astype(o_ref.dtype)


def _matmul(a, w, col_block0, n_out, out_dtype, tm, tn, name, side=()):
    m, k = a.shape
    out_bytes = jnp.dtype(out_dtype).itemsize
    grid = (n_out // tn, m // tm)
    side_in, side_out, side_shapes, side_vmem = _side_cast_plan(side, grid[1], grid[0] * grid[1])
    vmem = 2 * k * tn * 2 + 2 * tm * k * 2 + 2 * tm * tn * out_bytes + tm * tn * 4 + side_vmem
    out, *cast = pl.pallas_call(
        functools.partial(_mm_plain_kernel, side_t=tuple(s[2] for s in side)),
        grid=grid,
        in_specs=[pl.BlockSpec((tm, k), lambda j, i: (i, 0)),
                  pl.BlockSpec((k, tn), lambda j, i: (0, j + col_block0))] + side_in,
        out_specs=[pl.BlockSpec((tm, tn), lambda j, i: (i, j))] + side_out,
        out_shape=[jax.ShapeDtypeStruct((m, n_out), out_dtype)] + side_shapes,
        compiler_params=_cparams(2, vmem),
        name=name,
    )(a, w, *[s[0] for s in side])
    return out, cast


def _mm_resid_kernel(*refs, n_parts, side_t):
    n_side = len(side_t)
    a_refs = refs[:n_parts]
    w_refs = refs[n_parts:2 * n_parts]
    x_ref, gate_ref, lgate_ref = refs[2 * n_parts:2 * n_parts + 3]
    side_in = refs[2 * n_parts + 3:2 * n_parts + 3 + n_side]
    o_ref = refs[2 * n_parts + 3 + n_side]
    side_out = refs[2 * n_parts + 4 + n_side:]
    _side_cast(side_in, side_out, side_t)
    y = _dot(a_refs[0][...], w_refs[0][...])
    for a_ref, w_ref in zip(a_refs[1:], w_refs[1:]):
        y = y + _dot(a_ref[...], w_ref[...])
    o_ref[...] = x_ref[...] + (gate_ref[0] + lgate_ref[0]) * y


def _matmul_residual(a_parts, w, x, mod_rows, layer_rows, layer, gate_idx, seq, tm, tn, name, side=()):
    n_parts = len(a_parts)
    m, kp = a_parts[0].shape
    n = w.shape[1]
    per_seq = seq // tm
    grid = (n // tn, m // tm)
    side_in, side_out, side_shapes, side_vmem = _side_cast_plan(side, grid[1], grid[0] * grid[1])
    vmem = n_parts * (2 * kp * tn * 2 + 2 * tm * kp * 2) + 5 * tm * tn * 4 + side_vmem
    a_specs = [pl.BlockSpec((tm, kp), lambda j, i: (i, 0)) for _ in range(n_parts)]
    w_specs = [pl.BlockSpec((kp, tn), functools.partial(lambda j, i, p: (p, j), p=p))
               for p in range(n_parts)]
    out, *cast = pl.pallas_call(
        functools.partial(_mm_resid_kernel, n_parts=n_parts, side_t=tuple(s[2] for s in side)),
        grid=grid,
        in_specs=a_specs + w_specs + [
            pl.BlockSpec((tm, tn), lambda j, i: (i, j)),
            pl.BlockSpec((1, 1, tn), lambda j, i: ((i // per_seq) * N_MOD + gate_idx, 0, j)),
            pl.BlockSpec((1, 1, tn), lambda j, i: (layer * N_MOD + gate_idx, 0, j))] + side_in,
        out_specs=[pl.BlockSpec((tm, tn), lambda j, i: (i, j))] + side_out,
        out_shape=[jax.ShapeDtypeStruct((m, n), F32)] + side_shapes,
        compiler_params=_cparams(2, vmem),
        name=name,
    )(*a_parts, *([w] * n_parts), x, mod_rows, layer_rows, *[s[0] for s in side])
    return out, cast


HALO = 8
FFN_SUB_ROWS = 256
GDN_SUB_ROWS = 128


def _conv_begin(ybuf_ref, tm, tiles_per_seq):
    starts_sequence = (pl.program_id(1) % tiles_per_seq) == 0

    @pl.when(starts_sequence)
    def _():
        ybuf_ref[0:HALO, :] = jnp.zeros((HALO, ybuf_ref.shape[1]), F32)

    @pl.when(jnp.logical_not(starts_sequence))
    def _():
        ybuf_ref[0:HALO, :] = ybuf_ref[tm:tm + HALO, :]


def _conv_rows(y, ybuf_ref, cw_ref, r0):
    rows = y.shape[0]
    taps = cw_ref.shape[0]
    ybuf_ref[HALO + r0:HALO + r0 + rows, :] = y
    out = cw_ref[taps - 1:taps, :] * y
    for j in range(taps - 1):
        off = HALO + r0 - (taps - 1) + j
        out = out + cw_ref[j:j + 1, :] * ybuf_ref[off:off + rows, :]
    return out


def _mm_ffn_up_kernel(*refs, tiles_per_seq, sub, side_t):
    n_side = len(side_t)
    a_ref, wg_ref, wv_ref, cw_ref, cb_ref = refs[:5]
    side_in = refs[5:5 + n_side]
    o_ref = refs[5 + n_side]
    side_out = refs[6 + n_side:6 + 2 * n_side]
    ybuf_ref = refs[6 + 2 * n_side]
    _side_cast(side_in, side_out, side_t)
    tm = a_ref.shape[0]
    _conv_begin(ybuf_ref, tm, tiles_per_seq)
    for r0 in range(0, tm, sub):
        a = a_ref[r0:r0 + sub, :]
        gate = _conv_rows(_dot(a, wg_ref[...]), ybuf_ref, cw_ref, r0) + cb_ref[...]
        o_ref[r0:r0 + sub, :] = (_gelu(gate) * _dot(a, wv_ref[...])).astype(o_ref.dtype)


def _ffn_up(h, w_up, conv_w, conv_b, seq, tm, tn, side=()):
    m, k = h.shape
    d_ff = w_up.shape[1] // 2
    nb = d_ff // tn
    sub = _tile(tm, FFN_SUB_ROWS, 8)
    grid = (nb, m // tm)
    side_in, side_out, side_shapes, side_vmem = _side_cast_plan(side, grid[1], grid[0] * grid[1])
    vmem = 2 * 2 * k * tn * 2 + 2 * tm * k * 2 + 3 * tm * tn * 4 + 8 * sub * tn * 4 + side_vmem
    out, *cast = pl.pallas_call(
        functools.partial(_mm_ffn_up_kernel, tiles_per_seq=seq // tm, sub=sub,
                          side_t=tuple(s[2] for s in side)),
        grid=grid,
        in_specs=[pl.BlockSpec((tm, k), lambda j, i: (i, 0)),
                  pl.BlockSpec((k, tn), lambda j, i: (0, j)),
                  pl.BlockSpec((k, tn), lambda j, i: (0, j + nb)),
                  pl.BlockSpec((conv_w.shape[0], tn), lambda j, i: (0, j)),
                  pl.BlockSpec((1, tn), lambda j, i: (0, j))] + side_in,
        out_specs=[pl.BlockSpec((tm, tn), lambda j, i: (i, j))] + side_out,
        out_shape=[jax.ShapeDtypeStruct((m, d_ff), BF16)] + side_shapes,
        scratch_shapes=[pltpu.VMEM((tm + HALO, tn), F32)],
        compiler_params=_cparams(2, vmem),
        name="ffn_up",
    )(h, w_up, w_up, conv_w, conv_b.reshape(1, d_ff), *[s[0] for s in side])
    return out, cast


def _mm_gdn_proj_kernel(*refs, kinds, tiles_per_seq, sub):
    n = len(kinds)
    n_conv = sum(kind[0] != "plain" for kind in kinds)
    a_ref = refs[0]
    w_refs = refs[1:1 + n]
    cw_refs = iter(refs[1 + n:1 + n + n_conv])
    o_refs = refs[1 + n + n_conv:1 + 2 * n + n_conv]
    ybuf_refs = iter(refs[1 + 2 * n + n_conv:])
    conv_refs = [None if kind[0] == "plain" else (next(cw_refs), next(ybuf_refs)) for kind in kinds]
    tm = a_ref.shape[0]
    for conv in conv_refs:
        if conv is not None:
            _conv_begin(conv[1], tm, tiles_per_seq)
    for r0 in range(0, tm, sub):
        a = a_ref[r0:r0 + sub, :]
        for kind, w_ref, conv, o_ref in zip(kinds, w_refs, conv_refs, o_refs):
            y = _dot(a, w_ref[...])
            if conv is not None:
                y = _silu(_conv_rows(y, conv[1], conv[0], r0))
            if kind[0] != "l2":
                o_ref[r0:r0 + sub, :] = y.astype(o_ref.dtype)
                continue
            for g in range(y.shape[1] // HEAD_DIM):
                cols = slice(g * HEAD_DIM, (g + 1) * HEAD_DIM)
                yg = y[:, cols]
                inv = lax.rsqrt(jnp.sum(yg * yg, axis=-1, keepdims=True) + EPS)
                o_ref[r0:r0 + sub, cols] = (yg * (inv * kind[1])).astype(o_ref.dtype)


def _gdn_proj(h, w_in, conv_w, streams, n_out, seq, tm, tn, name):
    m, k = h.shape
    sub = _tile(tm, GDN_SUB_ROWS, 8)
    n = len(streams)
    conv_streams = [s for s in streams if s[1][0] != "plain"]
    vmem = n * (2 * k * tn * 2 + 3 * tm * tn * 4 + 8 * sub * tn * 4) + 2 * tm * k * 2
    w_specs = [pl.BlockSpec((k, tn), functools.partial(lambda j, i, c0: (0, j + c0), c0=s[0]))
               for s in streams]
    cw_specs = [pl.BlockSpec((conv_w.shape[0], tn), functools.partial(lambda j, i, c0: (0, j + c0), c0=s[0]))
                for s in conv_streams]
    out_spec = pl.BlockSpec((tm, tn), lambda j, i: (i, j))
    return pl.pallas_call(
        functools.partial(_mm_gdn_proj_kernel, kinds=tuple(s[1] for s in streams),
                          tiles_per_seq=seq // tm, sub=sub),
        grid=(n_out // tn, m // tm),
        in_specs=[pl.BlockSpec((tm, k), lambda j, i: (i, 0))] + w_specs + cw_specs,
        out_specs=[out_spec] * n,
        out_shape=[jax.ShapeDtypeStruct((m, n_out), s[2]) for s in streams],
        scratch_shapes=[pltpu.VMEM((tm + HALO, tn), F32) for _ in conv_streams],
        compiler_params=_cparams(2, vmem),
        name=name,
    )(h, *([w_in] * n), *([conv_w] * len(conv_streams)))


def _chunk_masks(blk, ch):
    shift = ch.bit_length() - 1
    r = lax.broadcasted_iota(jnp.int32, (blk, blk), 0)
    c = lax.broadcasted_iota(jnp.int32, (blk, blk), 1)
    same_chunk = jnp.right_shift(r, shift) == jnp.right_shift(c, shift)
    return (jnp.logical_and(same_chunk, r >= c), jnp.logical_and(same_chunk, r > c), same_chunk)


def _mm_gates_kernel(a_ref, w_ref, alog_ref, dt_ref, o_ref, *, n_heads):
    tm = a_ref.shape[0]
    blk = GDN_BLOCK
    lower_incl, _, same_chunk = _chunk_masks(blk, GDN_CHUNK)
    cum_mat = lower_incl.astype(BF16)
    tot_mat = same_chunk.astype(BF16)
    lane = lax.broadcasted_iota(jnp.int32, (blk, HEAD_DIM), 1)
    w_ab = w_ref[...]
    w = jnp.concatenate([w_ab, w_ab[:n_heads],
                         jnp.zeros((HEAD_DIM - 3 * n_heads, w_ab.shape[1]), F32)], axis=0).astype(BF16)
    for r0 in range(0, tm, blk):
        y = _dot_nt(a_ref[r0:r0 + blk, :], w)
        g = -jnp.exp(alog_ref[...]) * _softplus(y + dt_ref[...])
        hi, mid, lo = _split3(g)
        cum = _dot(cum_mat, hi) + _dot(cum_mat, mid) + _dot(cum_mat, lo)
        tot = _dot(tot_mat, hi) + _dot(tot_mat, mid) + _dot(tot_mat, lo)
        o_ref[r0:r0 + blk, :] = jnp.where(lane < n_heads, cum,
                                          jnp.where(lane < 2 * n_heads, _sigmoid(y), tot))


def _gdn_gates(h, w_in_t, layer, a_log, dt_bias, tm):
    m, k = h.shape
    n_heads = a_log.shape[0]
    tn = HEAD_DIM
    n_rows = w_in_t.shape[1]
    assert 3 * n_heads <= tn and tm % GDN_BLOCK == 0 and n_rows % (2 * n_heads) == 0
    zeros_h = jnp.zeros((n_heads,), F32)
    pad = lambda v: jnp.concatenate([v, zeros_h, v, jnp.zeros((tn - 3 * n_heads,), F32)]).reshape(1, tn)
    vmem = 4 * tn * k * 4 + 2 * tm * k * 2 + 4 * tm * tn * 4
    return pl.pallas_call(
        functools.partial(_mm_gates_kernel, n_heads=n_heads),
        grid=(m // tm,),
        in_specs=[pl.BlockSpec((tm, k), lambda i: (i, 0)),
                  pl.BlockSpec((None, 2 * n_heads, k), lambda i: (layer, n_rows // (2 * n_heads) - 1, 0)),
                  pl.BlockSpec((1, tn), lambda i: (0, 0)),
                  pl.BlockSpec((1, tn), lambda i: (0, 0))],
        out_specs=pl.BlockSpec((tm, tn), lambda i: (i, 0)),
        out_shape=jax.ShapeDtypeStruct((m, tn), F32),
        compiler_params=_cparams(1, vmem),
        name="gdn_gates",
    )(h, w_in_t, pad(a_log), pad(dt_bias))


def _sb_kernel(q_ref, k_ref, v_ref, o_ref, *, tq, n_pack, scale):
    qi = pl.program_id(2)
    row = lax.broadcasted_iota(jnp.int32, (tq, tq), 0)
    col = lax.broadcasted_iota(jnp.int32, (tq, tq), 1)
    later = (row > col).astype(BF16)
    below_diag = col < row
    head_cols = [slice(p * HEAD_DIM, (p + 1) * HEAD_DIM) for p in range(n_pack)]
    qs = [q_ref[:, cols] for cols in head_cols]

    def key_tile(kb, cs, accs, on_diagonal):
        start = pl.multiple_of(kb * tq, tq)
        keep = (lambda x: jnp.where(below_diag, x, 0.0)) if on_diagonal else (lambda x: x)
        heads = range(n_pack)
        z = [_dot_nt(qs[p], k_ref[pl.ds(start, tq), head_cols[p]]) * scale for p in heads]
        sp = [jnp.maximum(z[p], 0.0) + jnp.log(1.0 + jnp.exp(-jnp.abs(z[p]))) for p in heads]
        log_beta = [z[p] - sp[p] for p in heads]
        log_1m = [keep(-sp[p]) for p in heads]
        tail = [_dot(log_1m[p].astype(BF16), later) + cs[p] for p in heads]
        w = [keep(jnp.exp(log_beta[p] + tail[p])).astype(BF16) for p in heads]
        new_accs = [accs[p] + _dot(w[p], v_ref[pl.ds(start, tq), head_cols[p]]) for p in heads]
        new_cs = [cs[p] + jnp.sum(log_1m[p], axis=1, keepdims=True) for p in heads]
        c_max = functools.reduce(jnp.maximum, [jnp.max(c) for c in new_cs])
        return kb - 1, tuple(new_cs), tuple(new_accs), c_max > -F32_EXP_UNDERFLOW

    def cond(carry):
        kb, _, _, live = carry
        return jnp.logical_and(kb >= 0, live)

    zeros = lambda width: tuple(jnp.zeros((tq, width), F32) for _ in range(n_pack))
    carry = key_tile(qi, zeros(1), zeros(HEAD_DIM), True)
    _, _, accs, _ = lax.while_loop(cond, lambda c: key_tile(c[0], c[1], c[2], False), carry)
    for cols, acc in zip(head_cols, accs):
        o_ref[:, cols] = acc.astype(o_ref.dtype)


def _stick_breaking(proj, bsz, seq, n_heads, tq):
    t = proj.shape[0]
    nq = seq // tq
    n_pack = SB_HEAD_PACK if n_heads % SB_HEAD_PACK == 0 else 1
    groups = n_heads // n_pack
    width = n_pack * HEAD_DIM
    vmem = 2 * 2 * seq * width * 2 + 4 * tq * width * 2 + n_pack * 12 * tq * tq * 4
    return pl.pallas_call(
        functools.partial(_sb_kernel, tq=tq, n_pack=n_pack, scale=HEAD_DIM ** -0.5),
        grid=(bsz, groups, nq),
        in_specs=[pl.BlockSpec((tq, width), lambda b, h, i: (b * nq + i, h)),
                  pl.BlockSpec((seq, width), lambda b, h, i: (b, groups + h)),
                  pl.BlockSpec((seq, width), lambda b, h, i: (b, 2 * groups + h))],
        out_specs=pl.BlockSpec((tq, width), lambda b, h, i: (b * nq + i, h)),
        out_shape=jax.ShapeDtypeStruct((t, n_heads * HEAD_DIM), BF16),
        compiler_params=_cparams(3, vmem),
        name="stick_breaking",
    )(proj, proj, proj)


def _sgu_kernel(u_ref, v_ref, gain_ref, w_ref, b_ref, o_ref):
    rows = u_ref.shape[0]
    r = lax.broadcasted_iota(jnp.int32, (SGU_LEN, SGU_LEN), 0)
    c = lax.broadcasted_iota(jnp.int32, (SGU_LEN, SGU_LEN), 1)
    w = jnp.where(r >= c, w_ref[0], 0.0).astype(BF16)
    bias = b_ref[0]
    gain = gain_ref[0]
    for n in range(rows // SGU_LEN):
        sl = slice(n * SGU_LEN, (n + 1) * SGU_LEN)
        u = _gelu(u_ref[sl, :].astype(F32))
        v = _gelu(v_ref[sl, :].astype(F32))
        v = v * lax.rsqrt(jnp.mean(v * v, axis=-1, keepdims=True) + EPS) * gain
        mixed = _dot(w, v.astype(BF16)) + bias
        o_ref[sl, :] = (u * mixed).astype(o_ref.dtype)


def _spatial_gating(proj, sgu_gain, sgu_w, sgu_b, n_groups, u_block0, rows):
    t = proj.shape[0]
    return pl.pallas_call(
        _sgu_kernel,
        grid=(n_groups, t // rows),
        in_specs=[pl.BlockSpec((rows, HEAD_DIM), lambda g, i: (i, u_block0 + g)),
                  pl.BlockSpec((rows, HEAD_DIM), lambda g, i: (i, u_block0 + n_groups + g)),
                  pl.BlockSpec((1, 1, HEAD_DIM), lambda g, i: (g, 0, 0)),
                  pl.BlockSpec((1, SGU_LEN, SGU_LEN), lambda g, i: (g, 0, 0)),
                  pl.BlockSpec((1, SGU_LEN, 1), lambda g, i: (g, 0, 0))],
        out_specs=pl.BlockSpec((rows, HEAD_DIM), lambda g, i: (i, g)),
        out_shape=jax.ShapeDtypeStruct((t, n_groups * HEAD_DIM), BF16),
        compiler_params=_cparams(2, 16 * rows * HEAD_DIM * 4),
        name="spatial_gating",
    )(proj, proj, sgu_gain.reshape(n_groups, 1, HEAD_DIM), sgu_w,
      sgu_b.reshape(n_groups, SGU_LEN, 1))


def _neumann_solve(p_mats, xs, n_factors):
    idx = range(len(xs))
    n = p_mats[0].shape[0]
    eye = (lax.broadcasted_iota(jnp.int32, (n, n), 0)
           == lax.broadcasted_iota(jnp.int32, (n, n), 1)).astype(F32)
    ts = [eye + p_mats[i] for i in idx]
    for _ in range(1, n_factors):
        pb = [p_mats[i].astype(BF16) for i in idx]
        p_mats = [_dot(pb[i], pb[i]) for i in idx]
        ts = [ts[i] + _dot(p_mats[i].astype(BF16), ts[i].astype(BF16)) for i in idx]
    return [_dot(ts[i].astype(BF16), xs[i].astype(BF16)) for i in idx]


def _gdn_kernel(q_ref, k_ref, v_ref, z_ref, gb_ref, gain_ref, o_ref, state_ref, *, n_heads, n_pack):
    hg = pl.program_id(1)
    n_rows = q_ref.shape[0]
    blk, ch = GDN_BLOCK, GDN_CHUNK
    n_ch = blk // ch
    n_factors = ch.bit_length() - 1

    @pl.when(pl.program_id(2) == 0)
    def _():
        state_ref[...] = jnp.zeros(state_ref.shape, F32)

    lower_incl, lower_strict, _ = _chunk_masks(blk, ch)
    lane = lax.broadcasted_iota(jnp.int32, (blk, HEAD_DIM), 1)
    gain = gain_ref[...]

    def pick(gb, lane_idx):
        return jnp.sum(jnp.where(lane == lane_idx, gb, 0.0), axis=1, keepdims=True)

    head_cols = [slice(p * HEAD_DIM, (p + 1) * HEAD_DIM) for p in range(n_pack)]
    heads = range(n_pack)

    def block(b, states):
        rows = pl.ds(pl.multiple_of(b * blk, blk), blk)
        gb = gb_ref[rows, :]
        q = [q_ref[rows, cols].astype(F32) for cols in head_cols]
        k = [k_ref[rows, cols].astype(F32) for cols in head_cols]
        v = [v_ref[rows, cols].astype(F32) for cols in head_cols]
        gc = [pick(gb, hg * n_pack + p) for p in heads]
        beta = [pick(gb, hg * n_pack + p + n_heads) for p in heads]
        gl = [pick(gb, hg * n_pack + p + 2 * n_heads) for p in heads]
        gc_rows = [jnp.broadcast_to(g, (blk, HEAD_DIM)).T[0:1, :] for g in gc]
        diff = [gc[p] - gc_rows[p] for p in heads]
        decay = [jnp.exp(jnp.where(lower_incl, d, -1e30)) for d in diff]
        eg = [jnp.exp(g) for g in gc]
        kb = [k[p] * beta[p] for p in heads]
        kbf = [k[p].astype(BF16) for p in heads]
        p_mat = [jnp.where(lower_strict, _dot_nt((-kb[p]).astype(BF16), kbf[p]) * decay[p], 0.0)
                 for p in heads]
        qk = [(_dot_nt(q[p].astype(BF16), kbf[p]) * decay[p]).astype(BF16) for p in heads]
        x = [jnp.concatenate([v[p] * beta[p], kb[p] * eg[p]], axis=1) for p in heads]
        x = _neumann_solve(p_mat, x, n_factors)
        u = [x[p][:, :HEAD_DIM] for p in heads]
        w = [x[p][:, HEAD_DIM:].astype(BF16) for p in heads]
        q_dec = [(q[p] * eg[p]).astype(BF16) for p in heads]
        k_dec = [(k[p] * jnp.exp(gl[p] - gc[p])).astype(BF16) for p in heads]
        g_tot = [jnp.exp(gl[p]) for p in heads]

        states = list(states)
        o_inter = [[] for _ in heads]
        v_new = [[] for _ in heads]
        for ci in range(n_ch):
            sl = slice(ci * ch, (ci + 1) * ch)
            sb = [states[p].astype(BF16) for p in heads]
            v_c = [u[p][sl] - _dot(w[p][sl], sb[p]) for p in heads]
            for p in heads:
                o_inter[p].append(_dot(q_dec[p][sl], sb[p]))
                v_new[p].append(v_c[p])
            states = [states[p] * g_tot[p][ci * ch:ci * ch + 1, :]
                      + _dot_tn(k_dec[p][sl], v_c[p].astype(BF16)) for p in heads]
        o = [jnp.concatenate(o_inter[p], axis=0)
             + _dot(qk[p], jnp.concatenate(v_new[p], axis=0).astype(BF16)) for p in heads]
        for p, cols in enumerate(head_cols):
            on = o[p] * lax.rsqrt(jnp.mean(o[p] * o[p], axis=-1, keepdims=True) + EPS) * gain
            o_ref[rows, cols] = (on * _silu(z_ref[rows, cols].astype(F32))).astype(o_ref.dtype)
        return tuple(states)

    states = lax.fori_loop(0, n_rows // blk, block, tuple(state_ref[p] for p in range(n_pack)),
                           unroll=4)
    for p in range(n_pack):
        state_ref[p] = states[p]


def _gated_delta(q, k, v, z, gates, o_gain, bsz, seq, n_heads):
    t = q.shape[0]
    n_pack = GDN_HEAD_PACK if n_heads % GDN_HEAD_PACK == 0 else 1
    width = n_pack * HEAD_DIM
    rows = _tile(seq, 2048, GDN_BLOCK)
    tiles = seq // rows
    head_spec = pl.BlockSpec((rows, width), lambda b, h, r: (b * tiles + r, h))
    io_bytes = sum(jnp.dtype(a.dtype).itemsize for a in (q, k, v, z)) + 2
    vmem = 2 * (rows * width * io_bytes + rows * HEAD_DIM * 4) + n_pack * 40 * GDN_BLOCK * GDN_BLOCK * 4
    return pl.pallas_call(
        functools.partial(_gdn_kernel, n_heads=n_heads, n_pack=n_pack),
        grid=(bsz, n_heads // n_pack, tiles),
        in_specs=[head_spec, head_spec, head_spec, head_spec,
                  pl.BlockSpec((rows, HEAD_DIM), lambda b, h, r: (b * tiles + r, 0)),
                  pl.BlockSpec((1, HEAD_DIM), lambda b, h, r: (0, 0))],
        out_specs=head_spec,
        out_shape=jax.ShapeDtypeStruct((t, n_heads * HEAD_DIM), BF16),
        scratch_shapes=[pltpu.VMEM((n_pack, HEAD_DIM, HEAD_DIM), F32)],
        compiler_params=_cparams(3, vmem),
        name="gated_delta",
    )(q, k, v, z, gates, o_gain.reshape(1, HEAD_DIM))


def kernel(x, c, ada_w, ada_b, ada_layer, norm_mix, norm_ffn, norm_final, ev_w_in, ev_w_out, sgu_gain, sgu_w, sgu_b, gdn_w_in, gdn_conv, gdn_a_log, gdn_dt_bias, gdn_o_gain, gdn_w_out, ffn_w_up, ffn_conv, ffn_conv_b, ffn_w_down):
    bsz, seq, d = x.shape
    depth = ada_layer.shape[0]
    t = bsz * seq
    d_ff = ffn_w_down.shape[1]
    n_sb = d // 2 // HEAD_DIM
    n_sgu = sgu_gain.shape[1]
    n_gdn = gdn_a_log.shape[1]
    assert sgu_gain.shape[2] == HEAD_DIM and d == n_gdn * HEAD_DIM and sgu_w.shape[2] == SGU_LEN
    assert n_sgu == n_sb and seq % GDN_BLOCK == 0
    tm = _tile(seq, 1024)

    gdn_w_in_t = jnp.swapaxes(gdn_w_in, 1, 2)

    def mixer_weights(layer):
        if layer % 2 == 0:
            return (ev_w_in, layer // 2, False), (ev_w_out, layer // 2, False)
        return (gdn_w_in_t, layer // 2, True), (gdn_w_out, layer // 2, False)

    assert depth > 0
    w_mix_in, w_mix_out, w_up = _to_bf16(ev_w_in, 0), _to_bf16(ev_w_out, 0), _to_bf16(ffn_w_up, 0)

    mod_rows = _ada_project(c, ada_w, ada_b).reshape(bsz * N_MOD, 1, d)
    layer_rows = ada_layer.reshape(depth * N_MOD, 1, d)
    x = x.reshape(t, d)

    for layer in range(depth):
        h = _norm_modulate(x, norm_mix[layer], mod_rows, layer_rows, layer, 0, seq)
        if layer % 2 == 0:
            e = layer // 2
            n_proj = 5 * (d // 2)
            proj, early_cast = _matmul(h, w_mix_in, 0, n_proj, BF16, tm, _tile(n_proj, 1024),
                                       "even_in_proj",
                                       side=[mixer_weights(layer + 1)[0]] if layer + 1 < depth else [])
            o_a = _stick_breaking(proj, bsz, seq, n_sb, _tile(seq, 256))
            o_b = _spatial_gating(proj, sgu_gain[e], sgu_w[e], sgu_b[e], n_sgu, 3 * n_sb,
                                  _tile(seq, 2048))
            x, _ = _matmul_residual([o_a, o_b], w_mix_out, x, mod_rows, layer_rows, layer, 2, seq,
                                    tm, _tile(d, 512), "even_out_proj")
        else:
            o = layer // 2
            tn = _tile(d, 512)
            nb = d // tn
            conv_w = gdn_conv[o]
            q, k = _gdn_proj(h, w_mix_in, conv_w,
                             [(0, ("l2", HEAD_DIM ** -0.5), BF16), (nb, ("l2", 1.0), BF16)],
                             d, seq, tm, tn, "gdn_qk_proj")
            v, z = _gdn_proj(h, w_mix_in, conv_w,
                             [(2 * nb, ("silu",), F32), (3 * nb, ("plain",), F32)],
                             d, seq, tm, tn, "gdn_vz_proj")
            gates = _gdn_gates(h, gdn_w_in_t, o, gdn_a_log[o], gdn_dt_bias[o], tm)
            y = _gated_delta(q, k, v, z, gates, gdn_o_gain[o], bsz, seq, n_gdn)
            x, _ = _matmul_residual([y], w_mix_out, x, mod_rows, layer_rows, layer, 2, seq,
                                    tm, _tile(d, 512), "gdn_out_proj")
        has_next = layer + 1 < depth
        side = [(ffn_w_down, layer, False)]
        if has_next:
            next_in, next_out = mixer_weights(layer + 1)
            side += ([] if layer % 2 == 0 else [next_in]) + [next_out, (ffn_w_up, layer + 1, False)]
        h = _norm_modulate(x, norm_ffn[layer], mod_rows, layer_rows, layer, 3, seq)
        f, cast = _ffn_up(h, w_up, ffn_conv[layer], ffn_conv_b[layer], seq, tm, _tile(d_ff, 512),
                          side=side)
        w_down = cast[0]
        if has_next:
            w_mix_in = early_cast[0] if layer % 2 == 0 else cast[1]
            w_mix_out, w_up = cast[-2:]
        x, _ = _matmul_residual([f], w_down, x, mod_rows, layer_rows, layer, 5, seq,
                                _tile(seq, 512), _tile(d, 512), "ffn_down_proj")
    return _final_norm(x, norm_final).reshape(bsz, seq, d)
```
